```python
import math
import numpy as np
import jax
import jax.numpy as jnp
from jax import lax

D_MODEL = 1024
BATCH = 4
SEQ = 8192
DEPTH = 4

N_MIXERS = 3
D_FF = 2816
LN_EPS = 1e-5
ALPHA = (2 * DEPTH) ** 0.25
BETA = (8 * DEPTH) ** -0.25
GRID_W = 64
NA_HEADS = 16
NA_HEAD_DIM = D_MODEL // NA_HEADS
NA_ROWS = 8
NA_COLS = 16
NA_QB = 16
HGRN_HEAD_DIM = 128
HGRN_HEADS = D_MODEL // HGRN_HEAD_DIM
HGRN_CHUNK = 64
S5_GROUP_CH = 16
S5_STATE = 64
S5_GROUPS = D_MODEL // S5_GROUP_CH
S5_GROUP_BLOCK = 16
N_NA_LAYERS = (DEPTH + N_MIXERS - 1) // N_MIXERS
N_HGRN_LAYERS = (DEPTH + N_MIXERS - 2) // N_MIXERS
N_S5_LAYERS = (DEPTH + N_MIXERS - 3) // N_MIXERS

kernel_name = 'hybrid_na_hgrn2_s5_macaron_deepnorm_encoder'


def _layer_norm(z, g, b, out_dtype):
    z32 = z.astype(jnp.float32)
    mu = jnp.mean(z32, axis=-1, keepdims=True)
    var = jnp.mean(jnp.square(z32 - mu), axis=-1, keepdims=True)
    return ((z32 - mu) * lax.rsqrt(var + LN_EPS) * g + b).astype(out_dtype)


def _modulate(x, mods, k):
    return x * (1 + mods[:, 3 * k + 1, None, :]) + mods[:, 3 * k, None, :]


def _swiglu(h, w_in, w_out):
    a, b = jnp.split(h @ w_in, 2, axis=-1)
    return (jax.nn.silu(a) * b) @ w_out


def _na_column_tables():
    n_cb = GRID_W // NA_QB
    kb_w = NA_QB + NA_COLS
    cb = np.arange(n_cb)
    ks = np.clip(cb * NA_QB - NA_COLS // 2, 0, GRID_W - kb_w)
    key_cols = ks[:, None] + np.arange(kb_w)[None, :]
    q_cols = cb[:, None] * NA_QB + np.arange(NA_QB)[None, :]
    cs = np.clip(q_cols - NA_COLS // 2, 0, GRID_W - NA_COLS)
    kc = key_cols[:, None, :]
    col_mask = (kc >= cs[..., None]) & (kc < cs[..., None] + NA_COLS)
    dc_idx = np.clip(kc - q_cols[..., None] + NA_COLS - 1, 0, 2 * NA_COLS - 2)
    return key_cols, col_mask, dc_idx


def _neighbourhood_attention(h, w_qkv, rpb, w_out):
    bsz, seq, _ = h.shape
    rows = seq // GRID_W
    kh = min(NA_ROWS, rows)
    key_cols, col_mask, dc_idx = _na_column_tables()
    n_cb, kb_w = key_cols.shape
    qkv = (h @ w_qkv).reshape(bsz, rows, GRID_W, 3, NA_HEADS, NA_HEAD_DIM)
    qkv = qkv.transpose(3, 0, 4, 1, 2, 5)
    q = qkv[0] * NA_HEAD_DIM ** -0.5
    k = qkv[1]
    v = qkv[2]
    mask = jnp.asarray(np.broadcast_to(col_mask[:, :, None, :], (n_cb, NA_QB, kh, kb_w)).reshape(n_cb, NA_QB, kh * kb_w))
    dc = jnp.asarray(dc_idx)

    def row_block(r):
        rs = jnp.clip(r - kh // 2, 0, rows - kh)
        k_rows = lax.dynamic_slice_in_dim(k, rs, kh, axis=2)
        v_rows = lax.dynamic_slice_in_dim(v, rs, kh, axis=2)
        k_blk = k_rows[:, :, :, key_cols].transpose(0, 1, 3, 2, 4, 5).reshape(bsz, NA_HEADS, n_cb, kh * kb_w, NA_HEAD_DIM)
        v_blk = v_rows[:, :, :, key_cols].transpose(0, 1, 3, 2, 4, 5).reshape(bsz, NA_HEADS, n_cb, kh * kb_w, NA_HEAD_DIM)
        q_blk = lax.dynamic_index_in_dim(q, r, axis=2, keepdims=False).reshape(bsz, NA_HEADS, n_cb, NA_QB, NA_HEAD_DIM)
        s = jnp.einsum('bhnqd,bhnkd->bhnqk', q_blk, k_blk).astype(jnp.float32)
        dr_idx = rs + jnp.arange(kh) - r + NA_ROWS - 1
        bias = rpb[:, dr_idx[:, None, None, None], dc[None]]
        bias = bias.transpose(0, 2, 3, 1, 4).reshape(NA_HEADS, n_cb, NA_QB, kh * kb_w).astype(jnp.float32)
        s = jnp.where(mask, s + bias, -jnp.inf)
        p = jax.nn.softmax(s, axis=-1).astype(v.dtype)
        o = jnp.einsum('bhnqk,bhnkd->bhnqd', p, v_blk)
        return o.reshape(bsz, NA_HEADS, GRID_W, NA_HEAD_DIM)

    o = lax.map(row_block, jnp.arange(rows))
    o = o.transpose(1, 0, 3, 2, 4).reshape(bsz, seq, D_MODEL)
    return o @ w_out


def _hgrn_lower_bounds(lb_logits):
    p = jax.nn.softmax(lb_logits.astype(jnp.float32), axis=1)
    cs = jnp.cumsum(p, axis=1)
    return cs - cs[:, :1]


def _gla_chunked(q, k, v, g):
    bsz, seq, nh, dk = q.shape
    dv = v.shape[-1]
    nc = seq // HGRN_CHUNK

    def chunks(t):
        return t.reshape(bsz, nc, HGRN_CHUNK, nh, t.shape[-1]).transpose(1, 0, 3, 2, 4)

    q, k, v, g = chunks(q), chunks(k), chunks(v), chunks(g)
    b = jnp.cumsum(g, axis=3)
    b_ref = b[:, :, :, HGRN_CHUNK // 2:HGRN_CHUNK // 2 + 1]
    b_last = b[:, :, :, -1:]
    tri = jnp.tril(jnp.ones((HGRN_CHUNK, HGRN_CHUNK), dtype=bool))
    a = jnp.einsum('nbhtd,nbhsd->nbhts', q * jnp.exp(b - b_ref), k * jnp.exp(b_ref - b))
    a = jnp.where(tri, a, 0.0)
    o_intra = jnp.einsum('nbhts,nbhsv->nbhtv', a, v)
    q_dec = q * jnp.exp(b)
    k_dec = k * jnp.exp(b_last - b)
    decay = jnp.exp(b_last[:, :, :, 0])

    def step(state, inp):
        qd, kd, vc, dcy = inp
        o = jnp.einsum('bhtd,bhdv->bhtv', qd, state)
        state = dcy[..., None] * state + jnp.einsum('bhsd,bhsv->bhdv', kd, vc)
        return state, o

    state0 = jnp.zeros((bsz, nh, dk, dv), jnp.float32)
    _, o_inter = lax.scan(step, state0, (q_dec, k_dec, v, decay))
    return (o_intra + o_inter).transpose(1, 0, 3, 2, 4).reshape(bsz, seq, nh, dv)


def _hgrn2_mixer(h, w_in, lb_fwd, lb_bwd, norm_g, w_out):
    bsz, seq, _ = h.shape
    shp = (bsz, seq, HGRN_HEADS, HGRN_HEAD_DIM)
    q, i_in, f_fwd, f_bwd, gate = jnp.split(h @ w_in, 5, axis=-1)
    q = jax.nn.silu(q).reshape(shp).astype(jnp.float32)
    v = i_in.reshape(shp).astype(jnp.float32)

    def gates(f_logit, lb):
        lb = lb.reshape(HGRN_HEADS, HGRN_HEAD_DIM)
        f = lb + (1 - lb) * jax.nn.sigmoid(f_logit.reshape(shp).astype(jnp.float32))
        return 1 - f, jnp.log(f)

    k_f, g_f = gates(f_fwd, lb_fwd)
    o_f = _gla_chunked(q, k_f, v, g_f)
    k_b, g_b = gates(f_bwd, lb_bwd)
    o_b = jnp.flip(_gla_chunked(jnp.flip(q, 1), jnp.flip(k_b, 1), jnp.flip(v, 1), jnp.flip(g_b, 1)), 1)
    o = o_f + o_b
    o = o * lax.rsqrt(jnp.mean(jnp.square(o), axis=-1, keepdims=True) + LN_EPS)
    o = o.reshape(bsz, seq, D_MODEL) * norm_g * jax.nn.silu(gate.astype(jnp.float32))
    return o.astype(h.dtype) @ w_out


def _complex_affine_combine(e1, e2):
    a1r, a1i, b1r, b1i = e1
    a2r, a2i, b2r, b2i = e2
    return (a2r * a1r - a2i * a1i,
            a2r * a1i + a2i * a1r,
            a2r * b1r - a2i * b1i + b2r,
            a2r * b1i + a2i * b1r + b2i)


def _s5_scan(bu_re, bu_im, a_re, a_im, log_dt, reverse):
    a_re = a_re.astype(jnp.float32)
    a_im = a_im.astype(jnp.float32)
    dt = jnp.exp(log_dt.astype(jnp.float32))[:, None]
    mag = jnp.exp(a_re * dt)
    ang = a_im * dt
    ab_re = mag * jnp.cos(ang)
    ab_im = mag * jnp.sin(ang)
    den = a_re * a_re + a_im * a_im
    z_re = ((ab_re - 1) * a_re + ab_im * a_im) / den
    z_im = (ab_im * a_re - (ab_re - 1) * a_im) / den
    bb_re = z_re * bu_re - z_im * bu_im
    bb_im = z_re * bu_im + z_im * bu_re
    seq = bu_re.shape[0]
    lam_re = jnp.broadcast_to(ab_re, (seq, 1) + ab_re.shape)
    lam_im = jnp.broadcast_to(ab_im, (seq, 1) + ab_im.shape)
    _, _, x_re, x_im = lax.associative_scan(_complex_affine_combine, (lam_re, lam_im, bb_re, bb_im), reverse=reverse, axis=0)
    return x_re, x_im


def _s5_mixer(h, a_re, a_im, log_dt, b_re, b_im, c_re, c_im, d_skip, w_out):
    bsz, seq, _ = h.shape
    nb = S5_GROUPS // S5_GROUP_BLOCK
    u = h.astype(jnp.float32).reshape(bsz, seq, nb, S5_GROUP_BLOCK, S5_GROUP_CH).transpose(2, 1, 0, 3, 4)
    a_re_b = a_re.reshape(2, nb, S5_GROUP_BLOCK, S5_STATE).transpose(1, 0, 2, 3)
    a_im_b = a_im.reshape(2, nb, S5_GROUP_BLOCK, S5_STATE).transpose(1, 0, 2, 3)
    ldt_b = log_dt.reshape(2, nb, S5_GROUP_BLOCK).transpose(1, 0, 2)
    b_re_b = b_re.reshape(nb, S5_GROUP_BLOCK, S5_STATE, S5_GROUP_CH)
    b_im_b = b_im.reshape(nb, S5_GROUP_BLOCK, S5_STATE, S5_GROUP_CH)
    c_re_b = c_re.reshape(nb, S5_GROUP_BLOCK, S5_GROUP_CH, S5_STATE)
    c_im_b = c_im.reshape(nb, S5_GROUP_BLOCK, S5_GROUP_CH, S5_STATE)
    d_b = d_skip.reshape(nb, S5_GROUP_BLOCK, S5_GROUP_CH)

    def group_block(args):
        u_b, ar, ai, ldt, br, bi, cr, ci, dd = args
        bu_re = jnp.einsum('sbgc,gpc->sbgp', u_b, br.astype(jnp.float32))
        bu_im = jnp.einsum('sbgc,gpc->sbgp', u_b, bi.astype(jnp.float32))
        xf_re, xf_im = _s5_scan(bu_re, bu_im, ar[0], ai[0], ldt[0], False)
        xb_re, xb_im = _s5_scan(bu_re, bu_im, ar[1], ai[1], ldt[1], True)
        x_re = xf_re + xb_re
        x_im = xf_im + xb_im
        y = jnp.einsum('sbgp,gcp->sbgc', x_re, cr.astype(jnp.float32)) - jnp.einsum('sbgp,gcp->sbgc', x_im, ci.astype(jnp.float32))
        return y + dd * u_b

    y = lax.map(group_block, (u, a_re_b, a_im_b, ldt_b, b_re_b, b_im_b, c_re_b, c_im_b, d_b))
    y = y.transpose(2, 1, 0, 3, 4).reshape(bsz, seq, D_MODEL)
    y = jax.nn.gelu(y).astype(h.dtype)
    val, gte = jnp.split(y @ w_out, 2, axis=-1)
    return val * jax.nn.sigmoid(gte)


def setup_inputs(seed: int = 0) -> dict:
    key = jax.random.key(seed)
    ks = jax.random.split(key, 24)
    d = D_MODEL

    def nrm(k, shape, s):
        return jax.random.normal(k, shape, jnp.float32) * s

    s5_shape = (N_S5_LAYERS, 2, S5_GROUPS, S5_STATE)
    n_idx = jnp.arange(S5_STATE, dtype=jnp.float32)
    return {
        'x': nrm(ks[0], (BATCH, SEQ, d), 1.0),
        'c': nrm(ks[1], (BATCH, d), 1.0),
        'ada_w': nrm(ks[2], (DEPTH, d, 9 * d), d ** -0.5),
        'ada_b': nrm(ks[3], (DEPTH, 9 * d), 0.01),
        'ln_g': 1.0 + nrm(ks[4], (DEPTH, 3, d), 0.01),
        'ln_b': nrm(ks[5], (DEPTH, 3, d), 0.01),
        'ffn_w_in': nrm(ks[6], (DEPTH, 2, d, 2 * D_FF), d ** -0.5),
        'ffn_w_out': nrm(ks[7], (DEPTH, 2, D_FF, d), BETA * D_FF ** -0.5),
        'na_w_qkv': nrm(ks[8], (N_NA_LAYERS, d, 3 * d), d ** -0.5),
        'na_rpb': nrm(ks[9], (N_NA_LAYERS, NA_HEADS, 2 * NA_ROWS - 1, 2 * NA_COLS - 1), 0.1),
        'na_w_out': nrm(ks[10], (N_NA_LAYERS, d, d), BETA * d ** -0.5),
        'hgrn_w_in': nrm(ks[11], (N_HGRN_LAYERS, d, 5 * d), d ** -0.5),
        'hgrn_lb_logits': nrm(ks[12], (2, DEPTH, d), 0.1),
        'hgrn_norm_g': 1.0 + nrm(ks[13], (N_HGRN_LAYERS, d), 0.01),
        'hgrn_w_out': nrm(ks[14], (N_HGRN_LAYERS, d, d), BETA * d ** -0.5),
        's5_a_re': -0.5 + nrm(ks[15], s5_shape, 0.01),
        's5_a_im': math.pi * n_idx + nrm(ks[16], s5_shape, 0.01),
        's5_log_dt': jax.random.uniform(ks[17], (N_S5_LAYERS, 2, S5_GROUPS), jnp.float32, math.log(1e-3), math.log(1e-1)),
        's5_b_re': nrm(ks[18], (N_S5_LAYERS, S5_GROUPS, S5_STATE, S5_GROUP_CH), (2 * S5_GROUP_CH) ** -0.5),
        's5_b_im': nrm(ks[19], (N_S5_LAYERS, S5_GROUPS, S5_STATE, S5_GROUP_CH), (2 * S5_GROUP_CH) ** -0.5),
        's5_c_re': nrm(ks[20], (N_S5_LAYERS, S5_GROUPS, S5_GROUP_CH, S5_STATE), S5_STATE ** -0.5),
        's5_c_im': nrm(ks[21], (N_S5_LAYERS, S5_GROUPS, S5_GROUP_CH, S5_STATE), S5_STATE ** -0.5),
        's5_d': nrm(ks[22], (N_S5_LAYERS, d), 1.0),
        's5_w_out': nrm(ks[23], (N_S5_LAYERS, d, 2 * d), BETA * d ** -0.5),
    }


def reference(x, c, ada_w, ada_b, ln_g, ln_b, ffn_w_in, ffn_w_out, na_w_qkv, na_rpb, na_w_out,
              hgrn_w_in, hgrn_lb_logits, hgrn_norm_g, hgrn_w_out,
              s5_a_re, s5_a_im, s5_log_dt, s5_b_re, s5_b_im, s5_c_re, s5_c_im, s5_d, s5_w_out):
    dtype = x.dtype
    bsz = x.shape[0]
    lbs = _hgrn_lower_bounds(hgrn_lb_logits)
    for i in range(DEPTH):
        kind = i % N_MIXERS
        j = i // N_MIXERS
        mods = (jax.nn.silu(c) @ ada_w[i] + ada_b[i]).reshape(bsz, 9, D_MODEL)
        h = _modulate(x, mods, 0)
        y = _swiglu(h, ffn_w_in[i, 0], ffn_w_out[i, 0])
        x = _layer_norm(ALPHA * x + 0.5 * mods[:, 2, None, :] * y, ln_g[i, 0], ln_b[i, 0], dtype)
        h = _modulate(x, mods, 1)
        if kind == 0:
            y = _neighbourhood_attention(h, na_w_qkv[j], na_rpb[j], na_w_out[j])
        elif kind == 1:
            y = _hgrn2_mixer(h, hgrn_w_in[j], lbs[0, i], lbs[1, i], hgrn_norm_g[j], hgrn_w_out[j])
        else:
            y = _s5_mixer(h, s5_a_re[j], s5_a_im[j], s5_log_dt[j], s5_b_re[j], s5_b_im[j],
                          s5_c_re[j], s5_c_im[j], s5_d[j], s5_w_out[j])
        x = _layer_norm(ALPHA * x + mods[:, 5, None, :] * y, ln_g[i, 1], ln_b[i, 1], dtype)
        h = _modulate(x, mods, 2)
        y = _swiglu(h, ffn_w_in[i, 1], ffn_w_out[i, 1])
        x = _layer_norm(ALPHA * x + 0.5 * mods[:, 8, None, :] * y, ln_g[i, 2], ln_b[i, 2], dtype)
    return x
```

```python
import functools
import math

import numpy as np
import jax
import jax.numpy as jnp
from jax import lax
from jax.experimental import pallas as pl
from jax.experimental.pallas import tpu as pltpu

F32 = jnp.float32
BF16 = jnp.bfloat16

D_MODEL = 1024
DEPTH = 4
N_MIXERS = 3
D_FF = 2816
LN_EPS = 1e-5
ALPHA = (2 * DEPTH) ** 0.25
GRID_W = 64
NA_HEADS = 16
NA_HEAD_DIM = D_MODEL // NA_HEADS
NA_ROWS = 8
NA_COLS = 16
HGRN_HEAD_DIM = 128
HGRN_HEADS = D_MODEL // HGRN_HEAD_DIM
HGRN_CHUNK = 64
S5_GROUP_CH = 16
S5_STATE = 64
S5_GROUPS = D_MODEL // S5_GROUP_CH

V7X_VMEM_BYTES = 64 * 1024 * 1024
VMEM_LIMIT = 56 * 1024 * 1024
FFN_TM = 512
FFN_CHUNKS = 2
MM_TM = 512
MM_TN = 1024
OUT_TM = 512
ADA_TN = 1152
NA_RB = 8
GLA_CB = 4
S5_L = 32
S5_LC = S5_L * S5_GROUP_CH


def _params(sem, limit=VMEM_LIMIT):
    return pltpu.CompilerParams(dimension_semantics=sem, vmem_limit_bytes=limit)


def _layer_norm(z, g, b):
    mu = jnp.mean(z, axis=-1, keepdims=True)
    zc = z - mu
    var = jnp.mean(zc * zc, axis=-1, keepdims=True)
    return zc * lax.rsqrt(var + LN_EPS) * g + b


def _silu(a):
    return a * jax.nn.sigmoid(a)


def _ada_kernel(c_ref, w_ref, b_ref, o_ref):
    c = c_ref[...]
    sc = _silu(c).astype(BF16)
    w = w_ref[0].astype(BF16)
    o_ref[0] = jnp.dot(sc, w, preferred_element_type=F32) + b_ref[0]


def _ada_mods(c, ada_w, ada_b):
    bsz = c.shape[0]
    depth, d, n = ada_w.shape
    rows = 8
    cp = jnp.zeros((rows, d), F32).at[:bsz].set(c)
    out = pl.pallas_call(
        _ada_kernel,
        grid=(depth, n // ADA_TN),
        in_specs=[
            pl.BlockSpec((rows, d), lambda l, j: (0, 0)),
            pl.BlockSpec((1, d, ADA_TN), lambda l, j: (l, 0, j)),
            pl.BlockSpec((1, 1, ADA_TN), lambda l, j: (l, 0, j)),
        ],
        out_specs=pl.BlockSpec((1, rows, ADA_TN), lambda l, j: (l, 0, j)),
        out_shape=jax.ShapeDtypeStruct((depth, rows, n), F32),
        compiler_params=_params(("arbitrary", "arbitrary"), 32 * 1024 * 1024),
        name="ada",
    )(cp, ada_w, ada_b.reshape(depth, 1, n))
    return out[:, :bsz].reshape(depth, bsz, 9, d)


def _ffn_kernel(x_ref, m_ref, wa_ref, wb_ref, wo_ref, g_ref, b_ref, o_ref, h_scr, acc_scr, *, sub):
    k = pl.program_id(1)

    @pl.when(k == 0)
    def _():
        shift = m_ref[0, 3 * sub:3 * sub + 1, :]
        scale = m_ref[0, 3 * sub + 1:3 * sub + 2, :]
        h_scr[...] = (x_ref[...] * (1.0 + scale) + shift).astype(BF16)
        acc_scr[...] = jnp.zeros_like(acc_scr)

    h = h_scr[...]
    a = jnp.dot(h, wa_ref[...], preferred_element_type=F32)
    b = jnp.dot(h, wb_ref[...], preferred_element_type=F32)
    act = (_silu(a) * b).astype(BF16)
    acc_scr[...] += jnp.dot(act, wo_ref[...], preferred_element_type=F32)

    @pl.when(k == pl.num_programs(1) - 1)
    def _():
        gate = m_ref[0, 3 * sub + 2:3 * sub + 3, :]
        z = ALPHA * x_ref[...] + (0.5 * gate) * acc_scr[...]
        o_ref[...] = _layer_norm(z, g_ref[...], b_ref[...])


def _ffn(x, mods, w_in, w_out, ln_g, ln_b, sub, seq):
    t, d = x.shape
    dff = w_out.shape[0]
    fc = dff // FFN_CHUNKS
    tm = min(FFN_TM, seq)
    per_b = seq // tm
    return pl.pallas_call(
        functools.partial(_ffn_kernel, sub=sub),
        grid=(t // tm, FFN_CHUNKS),
        in_specs=[
            pl.BlockSpec((tm, d), lambda i, k: (i, 0)),
            pl.BlockSpec((1, 9, d), lambda i, k: (i // per_b, 0, 0)),
            pl.BlockSpec((d, fc), lambda i, k: (0, k)),
            pl.BlockSpec((d, fc), lambda i, k: (0, k + FFN_CHUNKS)),
            pl.BlockSpec((fc, d), lambda i, k: (k, 0)),
            pl.BlockSpec((1, d), lambda i, k: (0, 0)),
            pl.BlockSpec((1, d), lambda i, k: (0, 0)),
        ],
        out_specs=pl.BlockSpec((tm, d), lambda i, k: (i, 0)),
        out_shape=jax.ShapeDtypeStruct((t, d), F32),
        scratch_shapes=[pltpu.VMEM((tm, d), BF16), pltpu.VMEM((tm, d), F32)],
        compiler_params=_params(("parallel", "arbitrary")),
        name="ffn",
    )(x, mods, w_in, w_in, w_out, ln_g.reshape(1, d), ln_b.reshape(1, d))


def _mm_kernel(x_ref, m_ref, w_ref, o_ref, h_scr):
    @pl.when(pl.program_id(1) == 0)
    def _():
        shift = m_ref[0, 3:4, :]
        scale = m_ref[0, 4:5, :]
        h_scr[...] = (x_ref[...] * (1.0 + scale) + shift).astype(BF16)

    o_ref[...] = jnp.dot(h_scr[...], w_ref[...], preferred_element_type=F32).astype(o_ref.dtype)


def _mod_matmul(x, mods, w, out_dtype, seq):
    t, d = x.shape
    n = w.shape[1]
    tm = min(MM_TM, seq)
    per_b = seq // tm
    return pl.pallas_call(
        _mm_kernel,
        grid=(t // tm, n // MM_TN),
        in_specs=[
            pl.BlockSpec((tm, d), lambda i, j: (i, 0)),
            pl.BlockSpec((1, 9, d), lambda i, j: (i // per_b, 0, 0)),
            pl.BlockSpec((d, MM_TN), lambda i, j: (0, j)),
        ],
        out_specs=pl.BlockSpec((tm, MM_TN), lambda i, j: (i, j)),
        out_shape=jax.ShapeDtypeStruct((t, n), out_dtype),
        scratch_shapes=[pltpu.VMEM((tm, d), BF16)],
        compiler_params=_params(("parallel", "arbitrary"), 40 * 1024 * 1024),
        name="mm",
    )(x, mods, w)


def _mixer_epilogue(x_ref, m_ref, y, g_ref, b_ref, o_ref):
    gate = m_ref[0, 5:6, :]
    z = ALPHA * x_ref[...] + gate * y
    o_ref[...] = _layer_norm(z, g_ref[...], b_ref[...])


def _out_na_kernel(a_ref, x_ref, m_ref, w_ref, g_ref, b_ref, o_ref):
    y = jnp.dot(a_ref[...], w_ref[...], preferred_element_type=F32)
    _mixer_epilogue(x_ref, m_ref, y, g_ref, b_ref, o_ref)


def _out_hgrn_kernel(of_ref, ob_ref, gt_ref, x_ref, m_ref, ng_ref, w_ref, g_ref, b_ref, o_ref):
    o = of_ref[0] + ob_ref[0]
    parts = []
    for h in range(HGRN_HEADS):
        oh = o[:, h * HGRN_HEAD_DIM:(h + 1) * HGRN_HEAD_DIM]
        ms = jnp.mean(oh * oh, axis=-1, keepdims=True)
        parts.append(oh * lax.rsqrt(ms + LN_EPS))
    on = jnp.concatenate(parts, axis=1)
    lhs = (on * ng_ref[...] * _silu(gt_ref[...])).astype(BF16)
    y = jnp.dot(lhs, w_ref[...], preferred_element_type=F32)
    _mixer_epilogue(x_ref, m_ref, y, g_ref, b_ref, o_ref)


def _out_s5_kernel(a_ref, x_ref, m_ref, w_ref, g_ref, b_ref, o_ref):
    d = x_ref.shape[1]
    yy = jnp.dot(a_ref[...], w_ref[...], preferred_element_type=F32)
    y = yy[:, :d] * jax.nn.sigmoid(yy[:, d:])
    _mixer_epilogue(x_ref, m_ref, y, g_ref, b_ref, o_ref)


def _out_call(kernel, name, lead_specs, lead_args, x, mods, mid_specs, mid_args, w, ln_g, ln_b, seq):
    t, d = x.shape
    tm = min(OUT_TM, seq)
    per_b = seq // tm
    in_specs = list(lead_specs) + [
        pl.BlockSpec((tm, d), lambda i: (i, 0)),
        pl.BlockSpec((1, 9, d), lambda i: (i // per_b, 0, 0)),
    ] + list(mid_specs) + [
        pl.BlockSpec(w.shape, lambda i: (0, 0)),
        pl.BlockSpec((1, d), lambda i: (0, 0)),
        pl.BlockSpec((1, d), lambda i: (0, 0)),
    ]
    return pl.pallas_call(
        kernel,
        grid=(t // tm,),
        in_specs=in_specs,
        out_specs=pl.BlockSpec((tm, d), lambda i: (i, 0)),
        out_shape=jax.ShapeDtypeStruct((t, d), F32),
        compiler_params=_params(("parallel",), 40 * 1024 * 1024),
        name=name,
    )(*lead_args, x, mods, *mid_args, w, ln_g.reshape(1, d), ln_b.reshape(1, d))


def _na_bias_table(rpb):
    qc = np.arange(GRID_W)[:, None]
    kc = np.arange(GRID_W)[None, :]
    cs = np.clip(qc - NA_COLS // 2, 0, GRID_W - NA_COLS)
    mask = (kc >= cs) & (kc < cs + NA_COLS)
    dc = np.clip(kc - qc + NA_COLS - 1, 0, 2 * NA_COLS - 2)
    tbl = jnp.where(jnp.asarray(mask)[None, None], rpb[:, :, jnp.asarray(dc)].astype(F32), -jnp.inf)
    tbl2 = jnp.concatenate([tbl[:, :-1], tbl[:, 1:]], axis=-1)
    return tbl2.reshape(NA_HEADS * (2 * NA_ROWS - 2), GRID_W, 2 * GRID_W)


def _na_kernel(q_ref, kp_ref, kc_ref, kn_ref, vp_ref, vc_ref, vn_ref, tbl_ref, o_ref, k_scr, v_scr, *, rows):
    w = GRID_W
    half = NA_RB * w // 2
    full = NA_RB * w
    k_scr[0:half] = kp_ref[...]
    k_scr[half:half + full] = kc_ref[...]
    k_scr[half + full:] = kn_ref[...]
    v_scr[0:half] = vp_ref[...]
    v_scr[half:half + full] = vc_ref[...]
    v_scr[half + full:] = vn_ref[...]

    r0 = pl.program_id(1) * NA_RB
    lane = lax.broadcasted_iota(jnp.int32, (w, 2 * NA_HEAD_DIM), 1)
    lo = lane < NA_HEAD_DIM
    nd = 2 * NA_ROWS - 2

    def row_body(rr, carry):
        r = r0 + rr
        rs = jnp.clip(r - NA_ROWS // 2, 0, rows - NA_ROWS)
        woff = pl.multiple_of((rs - r0 + NA_ROWS // 2) * w, w)
        rel = r - rs
        qoff = pl.multiple_of(rr * w, w)
        outs = []
        for hp in range(NA_HEADS // 2):
            sl = slice(2 * NA_HEAD_DIM * hp, 2 * NA_HEAD_DIM * (hp + 1))
            q2 = q_ref[pl.ds(qoff, w), sl]
            k2 = k_scr[pl.ds(woff, NA_ROWS * w), sl]
            v2 = v_scr[pl.ds(woff, NA_ROWS * w), sl]
            res = []
            for hh in range(2):
                h = 2 * hp + hh
                msk = lo if hh == 0 else jnp.logical_not(lo)
                qm = jnp.where(msk, q2, jnp.zeros_like(q2))
                s = lax.dot_general(qm, k2, (((1,), (1,)), ((), ())), preferred_element_type=F32)
                bias = jnp.concatenate(
                    [tbl_ref[h * nd + 2 * j + (NA_ROWS - 1) - rel] for j in range(NA_ROWS // 2)], axis=1)
                s = s + bias
                m = jnp.max(s, axis=1, keepdims=True)
                p = jnp.exp(s - m)
                l = jnp.sum(p, axis=1, keepdims=True)
                o = jnp.dot(p.astype(BF16), v2, preferred_element_type=F32)
                res.append(o / l)
            outs.append(jnp.where(lo, res[0], res[1]))
        o_ref[pl.ds(qoff, w), :] = jnp.concatenate(outs, axis=1).astype(o_ref.dtype)
        return carry

    lax.fori_loop(0, NA_RB, row_body, 0)


def _na_core(qkv, tbl, bsz, seq):
    t = qkv.shape[0]
    d = D_MODEL
    rows = seq // GRID_W
    nrb = rows // NA_RB
    full = NA_RB * GRID_W
    half = full // 2

    def cur(col):
        return pl.BlockSpec((full, d), lambda b, i: (b * nrb + i, col))

    def prev(col):
        return pl.BlockSpec((half, d), lambda b, i: (b * 2 * nrb + jnp.maximum(2 * i - 1, 0), col))

    def nxt(col):
        return pl.BlockSpec((half, d), lambda b, i: (b * 2 * nrb + jnp.minimum(2 * i + 2, 2 * nrb - 1), col))

    return pl.pallas_call(
        functools.partial(_na_kernel, rows=rows),
        grid=(bsz, nrb),
        in_specs=[cur(0), prev(1), cur(1), nxt(1), prev(2), cur(2), nxt(2),
                  pl.BlockSpec(tbl.shape, lambda b, i: (0, 0, 0))],
        out_specs=pl.BlockSpec((full, d), lambda b, i: (b * nrb + i, 0)),
        out_shape=jax.ShapeDtypeStruct((t, d), BF16),
        scratch_shapes=[pltpu.VMEM((2 * full, d), BF16), pltpu.VMEM((2 * full, d), BF16)],
        compiler_params=_params(("parallel", "arbitrary")),
        name="na",
    )(qkv, qkv, qkv, qkv, qkv, qkv, qkv, tbl)


def _gla_kernel(q_ref, v_ref, f_ref, lb_ref, o_ref, st_scr):
    c = HGRN_CHUNK
    dh = HGRN_HEAD_DIM
    d = pl.program_id(1)
    is_f = d == 0

    @pl.when(pl.program_id(2) == 0)
    def _():
        st_scr[...] = jnp.zeros_like(st_scr)

    lbv = lb_ref[0]
    row = lax.broadcasted_iota(jnp.int32, (c, c), 0)
    col = lax.broadcasted_iota(jnp.int32, (c, c), 1)
    tri = (col - row) * (1 - 2 * d) <= 0
    tri_bf = tri.astype(F32).astype(BF16)

    def chunk_body(ci, carry):
        cc = jnp.where(is_f, ci, GLA_CB - 1 - ci)
        off = pl.multiple_of(cc * c, c)
        f = lbv + (1.0 - lbv) * jax.nn.sigmoid(f_ref[pl.ds(off, c), :])
        g = jnp.log(f)
        k = 1.0 - f
        g1 = g.astype(BF16)
        r1 = g - g1.astype(F32)
        g2 = r1.astype(BF16)
        g3 = (r1 - g2.astype(F32)).astype(BF16)
        bc = (jnp.dot(tri_bf, g1, preferred_element_type=F32)
              + jnp.dot(tri_bf, g2, preferred_element_type=F32)
              + jnp.dot(tri_bf, g3, preferred_element_type=F32))
        b_mid = jnp.where(is_f, bc[c // 2:c // 2 + 1], bc[c // 2 - 1:c // 2])
        b_last = jnp.where(is_f, bc[c - 1:c], bc[0:1])
        qs = _silu(q_ref[pl.ds(off, c), :])
        qe = (qs * jnp.exp(bc - b_mid)).astype(BF16)
        ke = (k * jnp.exp(b_mid - bc)).astype(BF16)
        qd = (qs * jnp.exp(bc)).astype(BF16)
        kd = (k * jnp.exp(b_last - bc)).astype(BF16)
        dec = jnp.exp(b_last)
        v = v_ref[pl.ds(off, c), :].astype(BF16)
        for h in range(HGRN_HEADS):
            sl = slice(h * dh, (h + 1) * dh)
            st = st_scr[sl, :]
            a = lax.dot_general(qe[:, sl], ke[:, sl], (((1,), (1,)), ((), ())), preferred_element_type=F32)
            a = jnp.where(tri, a, 0.0).astype(BF16)
            o = jnp.dot(a, v[:, sl], preferred_element_type=F32)
            o = o + lax.dot_general(qd[:, sl], st.astype(BF16), (((1,), (1,)), ((), ())),
                                    preferred_element_type=F32)
            upd = lax.dot_general(v[:, sl], kd[:, sl], (((0,), (0,)), ((), ())), preferred_element_type=F32)
            st_scr[sl, :] = dec[:, sl] * st + upd
            o_ref[0, pl.ds(off, c), sl] = o
        return carry

    lax.fori_loop(0, GLA_CB, chunk_body, 0)


def _gla(proj, lbs, bsz, seq):
    t = proj.shape[0]
    d = D_MODEL
    tb = GLA_CB * HGRN_CHUNK
    nblk = seq // tb

    def tok(b, dr, i):
        return b * nblk + i + dr * (nblk - 1 - 2 * i)

    return pl.pallas_call(
        _gla_kernel,
        grid=(bsz, 2, nblk),
        in_specs=[
            pl.BlockSpec((tb, d), lambda b, dr, i: (tok(b, dr, i), 0)),
            pl.BlockSpec((tb, d), lambda b, dr, i: (tok(b, dr, i), 1)),
            pl.BlockSpec((tb, d), lambda b, dr, i: (tok(b, dr, i), 2 + dr)),
            pl.BlockSpec((1, 1, d), lambda b, dr, i: (dr, 0, 0)),
        ],
        out_specs=pl.BlockSpec((1, tb, d), lambda b, dr, i: (dr, tok(b, dr, i), 0)),
        out_shape=jax.ShapeDtypeStruct((2, t, d), F32),
        scratch_shapes=[pltpu.VMEM((d, HGRN_HEAD_DIM), F32)],
        compiler_params=_params(("parallel", "parallel", "arbitrary"), 32 * 1024 * 1024),
        name="gla",
    )(proj, proj, proj, lbs.reshape(2, 1, d))


def _s5_coef_kernel(a_ref, bt_ref, c_ref, ke_ref, wt_ref, lam_ref):
    L = S5_L
    ch = S5_GROUP_CH
    p = S5_STATE
    hi = lax.Precision.HIGHEST
    a = a_ref[0]
    cr = c_ref[0, 0]
    ci = c_ref[0, 1]

    zb, lam, ang_dt = [], [], []
    for dr in range(2):
        ar = a[3 * dr:3 * dr + 1]
        ai = a[3 * dr + 1:3 * dr + 2]
        dt = jnp.exp(a[3 * dr + 2:3 * dr + 3])
        mag = jnp.exp(ar * dt)
        ang = ai * dt
        lr = mag * jnp.cos(ang)
        li = mag * jnp.sin(ang)
        den = ar * ar + ai * ai
        zr = ((lr - 1.0) * ar + li * ai) / den
        zi = (li * ar - (lr - 1.0) * ai) / den
        br = bt_ref[0, 0]
        bi = bt_ref[0, 1]
        zb.append((zr * br - zi * bi, zr * bi + zi * br))
        ang_dt.append((ar * dt, ang))

    def powers(dr, m):
        lre, lang = ang_dt[dr]
        mg = jnp.exp(m * lre)
        return mg * jnp.cos(m * lang), mg * jnp.sin(m * lang)

    def outer(pw, coef):
        pr, pi = pw
        fr, fi = coef
        r = pr.shape[0]
        re = pr[:, None, :] * fr[None] - pi[:, None, :] * fi[None]
        im = pr[:, None, :] * fi[None] + pi[:, None, :] * fr[None]
        return re.reshape(r * ch, p), im.reshape(r * ch, p)

    n = lax.broadcasted_iota(jnp.int32, (2 * L, 1), 0)
    kseq = jnp.zeros((ch, 2 * L * ch), F32)
    for dr in range(2):
        lag = (n - (L - 1)) if dr == 0 else ((L - 1) - n)
        valid = (lag >= 0).astype(F32)
        pr, pi = powers(dr, jnp.maximum(lag, 0).astype(F32))
        gr, gi = outer((pr * valid, pi * valid), (cr, ci))
        gmat = jnp.concatenate([gr, gi], axis=1)
        zmat = jnp.concatenate([zb[dr][0], -zb[dr][1]], axis=1)
        kseq = kseq + lax.dot_general(zmat, gmat, (((1,), (1,)), ((), ())),
                                      precision=hi, preferred_element_type=F32)
    width = 2 * L * ch
    for s in range(L):
        off = (L - 1 - s) * ch
        win = pltpu.roll(kseq, (width - off) % width, 1)[:, :L * ch]
        ke_ref[0, s * ch:(s + 1) * ch, 0:L * ch] = win.astype(ke_ref.dtype)

    srow = lax.broadcasted_iota(jnp.int32, (L, 1), 0).astype(F32)
    e0 = outer(powers(0, (L - 1) - srow), zb[0])
    e1 = outer(powers(1, srow), zb[1])
    ke_ref[0, :, L * ch:] = jnp.concatenate([e0[0], e0[1], e1[0], e1[1]], axis=1).astype(ke_ref.dtype)

    w0 = outer(powers(0, srow + 1.0), (cr, ci))
    w1 = outer(powers(1, L - srow), (cr, ci))
    wt_ref[0] = jnp.concatenate([w0[0], -w0[1], w1[0], -w1[1]], axis=1).astype(wt_ref.dtype)

    lf = powers(0, jnp.full((1, 1), float(L), F32))
    lb = powers(1, jnp.full((1, 1), float(L), F32))
    lam_ref[0] = jnp.concatenate([lf[0], lf[1], lb[0], lb[1]], axis=1)


def _s5_coefs(a_re, a_im, log_dt, b_re, b_im, c_re, c_im):
    g, p, ch = S5_GROUPS, S5_STATE, S5_GROUP_CH
    ldt = jnp.broadcast_to(log_dt[:, :, None], (2, g, p))
    zero = jnp.zeros((g, p), F32)
    amat = jnp.stack([a_re[0], a_im[0], ldt[0], a_re[1], a_im[1], ldt[1], zero, zero], axis=1)
    bt = jnp.stack([b_re, b_im], axis=1).transpose(0, 1, 3, 2)
    cc = jnp.stack([c_re, c_im], axis=1)
    return pl.pallas_call(
        _s5_coef_kernel,
        grid=(g,),
        in_specs=[
            pl.BlockSpec((1, 8, p), lambda i: (i, 0, 0)),
            pl.BlockSpec((1, 2, ch, p), lambda i: (i, 0, 0, 0)),
            pl.BlockSpec((1, 2, ch, p), lambda i: (i, 0, 0, 0)),
        ],
        out_specs=[
            pl.BlockSpec((1, S5_LC, S5_LC + 4 * p), lambda i: (i, 0, 0)),
            pl.BlockSpec((1, S5_LC, 4 * p), lambda i: (i, 0, 0)),
            pl.BlockSpec((1, 1, 4 * p), lambda i: (i, 0, 0)),
        ],
        out_shape=[
            jax.ShapeDtypeStruct((g, S5_LC, S5_LC + 4 * p), BF16),
            jax.ShapeDtypeStruct((g, S5_LC, 4 * p), BF16),
            jax.ShapeDtypeStruct((g, 1, 4 * p), F32),
        ],
        compiler_params=_params(("parallel",), 32 * 1024 * 1024),
        name="s5_coef",
    )(amat, bt, cc)


def _s5_in_kernel(x_ref, sc_ref, sh_ref, ke_ref, e_ref):
    u = x_ref[0] * (1.0 + sc_ref[0, 0]) + sh_ref[0, 0]
    e_ref[0] = jnp.dot(u.astype(BF16), ke_ref[0, :, S5_LC:], preferred_element_type=F32)


def _s5_scan_kernel(e_ref, lam_ref, s_ref):
    p = S5_STATE
    nch = e_ref.shape[1]
    lam = lam_ref[...]
    lane = lax.broadcasted_iota(jnp.int32, (1, 1, 2 * p), 2)
    sign = jnp.where(lane < p, -1.0, 1.0)
    coef = []
    for dr in range(2):
        lr = lam[:, :, 2 * p * dr:2 * p * dr + p]
        li = lam[:, :, 2 * p * dr + p:2 * p * (dr + 1)]
        coef.append((jnp.concatenate([lr, lr], axis=2), jnp.concatenate([li, li], axis=2) * sign))

    def step(x, e, cf):
        return cf[0] * x + cf[1] * pltpu.roll(x, p, 2) + e

    sub = 8

    def body(j, carry):
        xf, xb = carry
        of = pl.multiple_of(j * sub, sub)
        ob = pl.multiple_of(nch - sub - j * sub, sub)
        ef = e_ref[:, pl.ds(of, sub), 0:2 * p]
        eb = e_ref[:, pl.ds(ob, sub), 2 * p:]
        sf, sb = [], []
        for r in range(sub):
            sf.append(xf)
            xf = step(xf, ef[:, r:r + 1, :], coef[0])
        for r in range(sub - 1, -1, -1):
            sb.append(xb)
            xb = step(xb, eb[:, r:r + 1, :], coef[1])
        s_ref[:, pl.ds(of, sub), 0:2 * p] = jnp.concatenate(sf, axis=1)
        s_ref[:, pl.ds(ob, sub), 2 * p:] = jnp.concatenate(sb[::-1], axis=1)
        return xf, xb

    zero = jnp.zeros((e_ref.shape[0], 1, 2 * p), F32)
    lax.fori_loop(0, nch // sub, body, (zero, zero))


def _gelu_tanh(y):
    return 0.5 * y * (1.0 + jnp.tanh(math.sqrt(2.0 / math.pi) * (y + 0.044715 * (y * y * y))))


def _s5_out_kernel(x_ref, sc_ref, sh_ref, dk_ref, ke_ref, s_ref, wt_ref, y_ref):
    u = x_ref[0] * (1.0 + sc_ref[0, 0]) + sh_ref[0, 0]
    y = jnp.dot(u.astype(BF16), ke_ref[0, :, :S5_LC], preferred_element_type=F32)
    y = y + lax.dot_general(s_ref[0].astype(BF16), wt_ref[0], (((1,), (1,)), ((), ())),
                            preferred_element_type=F32)
    y = y + dk_ref[0] * u
    y_ref[0] = _gelu_tanh(y).astype(y_ref.dtype)


def _s5_core(x, mods, coefs, d_skip, bsz, seq):
    ke, wt, lam = coefs
    t, d = x.shape
    g, ch, L, p = S5_GROUPS, S5_GROUP_CH, S5_L, S5_STATE
    lc = S5_LC
    m = t // L
    mb = seq // L
    gb = 8

    xt = x.reshape(m, L, g, ch).transpose(2, 0, 1, 3).reshape(g, m, lc)

    def tile_mod(v):
        v = v.reshape(bsz, g, 1, ch)
        return jnp.broadcast_to(v[:, :, None], (bsz, g, L, 1, ch)).reshape(bsz, g, 1, lc)

    shift_t = tile_mod(mods[:, 3])
    scale_t = tile_mod(mods[:, 4])
    dk_t = jnp.broadcast_to(d_skip.reshape(g, 1, 1, ch), (g, 1, L, ch)).reshape(g, 1, lc)

    x_spec = pl.BlockSpec((1, mb, lc), lambda gi, b: (gi, b, 0))
    mod_spec = pl.BlockSpec((1, 1, 1, lc), lambda gi, b: (b, gi, 0, 0))
    ke_spec = pl.BlockSpec((1, lc, lc + 4 * p), lambda gi, b: (gi, 0, 0))

    e = pl.pallas_call(
        _s5_in_kernel,
        grid=(g, bsz),
        in_specs=[x_spec, mod_spec, mod_spec, ke_spec],
        out_specs=pl.BlockSpec((1, mb, 4 * p), lambda gi, b: (gi, b, 0)),
        out_shape=jax.ShapeDtypeStruct((g, m, 4 * p), F32),
        compiler_params=_params(("parallel", "parallel"), 32 * 1024 * 1024),
        name="s5_in",
    )(xt, scale_t, shift_t, ke)

    states = pl.pallas_call(
        _s5_scan_kernel,
        grid=(g // gb, bsz),
        in_specs=[
            pl.BlockSpec((gb, mb, 4 * p), lambda gi, b: (gi, b, 0)),
            pl.BlockSpec((gb, 1, 4 * p), lambda gi, b: (gi, 0, 0)),
        ],
        out_specs=pl.BlockSpec((gb, mb, 4 * p), lambda gi, b: (gi, b, 0)),
        out_shape=jax.ShapeDtypeStruct((g, m, 4 * p), F32),
        compiler_params=_params(("parallel", "parallel"), 32 * 1024 * 1024),
        name="s5_scan",
    )(e, lam)

    yt = pl.pallas_call(
        _s5_out_kernel,
        grid=(g, bsz),
        in_specs=[
            x_spec, mod_spec, mod_spec,
            pl.BlockSpec((1, 1, lc), lambda gi, b: (gi, 0, 0)),
            ke_spec,
            pl.BlockSpec((1, mb, 4 * p), lambda gi, b: (gi, b, 0)),
            pl.BlockSpec((1, lc, 4 * p), lambda gi, b: (gi, 0, 0)),
        ],
        out_specs=pl.BlockSpec((1, mb, lc), lambda gi, b: (gi, b, 0)),
        out_shape=jax.ShapeDtypeStruct((g, m, lc), BF16),
        compiler_params=_params(("parallel", "parallel"), 32 * 1024 * 1024),
        name="s5_out",
    )(xt, scale_t, shift_t, dk_t, ke, states, wt)

    return yt.reshape(g, m, L, ch).transpose(1, 2, 0, 3).reshape(t, d)


def _row_spec(tm, d, col=0):
    return pl.BlockSpec((tm, d), lambda i: (i, col))


def _na_mixer(x, mods, w_qkv, rpb, w_out, ln_g, ln_b, bsz, seq):
    d = D_MODEL
    scale = jnp.concatenate([jnp.full((d,), NA_HEAD_DIM ** -0.5, F32), jnp.ones((2 * d,), F32)])
    qkv = _mod_matmul(x, mods, (w_qkv * scale).astype(BF16), BF16, seq)
    att = _na_core(qkv, _na_bias_table(rpb), bsz, seq)
    tm = min(OUT_TM, seq)
    return _out_call(_out_na_kernel, "out_na", [_row_spec(tm, d)], [att], x, mods, [], [],
                     w_out.astype(BF16), ln_g, ln_b, seq)


def _hgrn_mixer(x, mods, w_in, lbs, norm_g, w_out, ln_g, ln_b, bsz, seq):
    d = D_MODEL
    proj = _mod_matmul(x, mods, w_in.astype(BF16), F32, seq)
    o2 = _gla(proj, lbs, bsz, seq)
    tm = min(OUT_TM, seq)
    lead = [pl.BlockSpec((1, tm, d), lambda i: (0, i, 0)),
            pl.BlockSpec((1, tm, d), lambda i: (1, i, 0)),
            _row_spec(tm, d, 4)]
    return _out_call(_out_hgrn_kernel, "out_hgrn", lead, [o2, o2, proj], x, mods,
                     [pl.BlockSpec((1, d), lambda i: (0, 0))], [norm_g.reshape(1, d)],
                     w_out.astype(BF16), ln_g, ln_b, seq)


def _s5_mixer(x, mods, coefs, d_skip, w_out, ln_g, ln_b, bsz, seq):
    d = D_MODEL
    act = _s5_core(x, mods, coefs, d_skip, bsz, seq)
    tm = min(OUT_TM, seq)
    return _out_call(_out_s5_kernel, "out_s5", [_row_spec(tm, d)], [act], x, mods, [], [],
                     w_out.astype(BF16), ln_g, ln_b, seq)


def _hgrn_lower_bounds(lb_logits):
    p = jax.nn.softmax(lb_logits.astype(F32), axis=1)
    cs = jnp.cumsum(p, axis=1)
    return cs - cs[:, :1]


def kernel(x, c, ada_w, ada_b, ln_g, ln_b, ffn_w_in, ffn_w_out, na_w_qkv, na_rpb, na_w_out,
           hgrn_w_in, hgrn_lb_logits, hgrn_norm_g, hgrn_w_out,
           s5_a_re, s5_a_im, s5_log_dt, s5_b_re, s5_b_im, s5_c_re, s5_c_im, s5_d, s5_w_out):
    bsz, seq, d = x.shape
    depth = ada_w.shape[0]
    mods_all = _ada_mods(c, ada_w, ada_b)
    lbs = _hgrn_lower_bounds(hgrn_lb_logits)
    xf = x.reshape(bsz * seq, d)
    for i in range(depth):
        kind = i % N_MIXERS
        j = i // N_MIXERS
        mods = mods_all[i]
        xf = _ffn(xf, mods, ffn_w_in[i, 0].astype(BF16), ffn_w_out[i, 0].astype(BF16),
                  ln_g[i, 0], ln_b[i, 0], 0, seq)
        if kind == 0:
            xf = _na_mixer(xf, mods, na_w_qkv[j], na_rpb[j], na_w_out[j], ln_g[i, 1], ln_b[i, 1], bsz, seq)
        elif kind == 1:
            xf = _hgrn_mixer(xf, mods, hgrn_w_in[j], lbs[:, i], hgrn_norm_g[j], hgrn_w_out[j],
                             ln_g[i, 1], ln_b[i, 1], bsz, seq)
        else:
            coefs = _s5_coefs(s5_a_re[j], s5_a_im[j], s5_log_dt[j], s5_b_re[j], s5_b_im[j],
                              s5_c_re[j], s5_c_im[j])
            xf = _s5_mixer(xf, mods, coefs, s5_d[j], s5_w_out[j], ln_g[i, 1], ln_b[i, 1], bsz, seq)
        xf = _ffn(xf, mods, ffn_w_in[i, 1].astype(BF16), ffn_w_out[i, 1].astype(BF16),
                  ln_g[i, 2], ln_b[i, 2], 2, seq)
    return xf.reshape(bsz, seq, d)
```

```python
import functools
import math

import numpy as np
import jax
import jax.numpy as jnp
from jax import lax
from jax.experimental import pallas as pl
from jax.experimental.pallas import tpu as pltpu

F32 = jnp.float32
BF16 = jnp.bfloat16

D_MODEL = 1024
DEPTH = 4
N_MIXERS = 3
D_FF = 2816
LN_EPS = 1e-5
ALPHA = (2 * DEPTH) ** 0.25
GRID_W = 64
NA_HEADS = 16
NA_HEAD_DIM = D_MODEL // NA_HEADS
NA_ROWS = 8
NA_COLS = 16
HGRN_HEAD_DIM = 128
HGRN_HEADS = D_MODEL // HGRN_HEAD_DIM
HGRN_CHUNK = 64
S5_GROUP_CH = 16
S5_STATE = 64
S5_GROUPS = D_MODEL // S5_GROUP_CH

V7X_VMEM_BYTES = 64 * 1024 * 1024
VMEM_LIMIT = 56 * 1024 * 1024
FFN_TM = 512
FFN_SPLIT = 2
MM_TM = 512
MM_TN = 1024
OUT_TM = 512
ADA_TN = 1152
NA_RB = 8
GLA_CB = 4
S5_L = 32
S5_LC = S5_L * S5_GROUP_CH
S5_SUB = 8
S5_SCAN_GROUPS = 4


def _params(sem, limit=VMEM_LIMIT):
    return pltpu.CompilerParams(dimension_semantics=sem, vmem_limit_bytes=limit)


def _layer_norm(z, g, b):
    mu = jnp.mean(z, axis=-1, keepdims=True)
    zc = z - mu
    var = jnp.mean(zc * zc, axis=-1, keepdims=True)
    return zc * lax.rsqrt(var + LN_EPS) * g + b


def _silu(a):
    return a * jax.nn.sigmoid(a)


def _ada_kernel(c_ref, w_ref, b_ref, o_ref):
    c = c_ref[...]
    sc = _silu(c).astype(BF16)
    w = w_ref[0].astype(BF16)
    o_ref[0] = jnp.dot(sc, w, preferred_element_type=F32) + b_ref[0]


def _ada_mods(c, ada_w, ada_b):
    bsz = c.shape[0]
    depth, d, n = ada_w.shape
    rows = 8
    cp = jnp.zeros((rows, d), F32).at[:bsz].set(c)
    out = pl.pallas_call(
        _ada_kernel,
        grid=(depth, n // ADA_TN),
        in_specs=[
            pl.BlockSpec((rows, d), lambda l, j: (0, 0)),
            pl.BlockSpec((1, d, ADA_TN), lambda l, j: (l, 0, j)),
            pl.BlockSpec((1, 1, ADA_TN), lambda l, j: (l, 0, j)),
        ],
        out_specs=pl.BlockSpec((1, rows, ADA_TN), lambda l, j: (l, 0, j)),
        out_shape=jax.ShapeDtypeStruct((depth, rows, n), F32),
        compiler_params=_params(("arbitrary", "arbitrary"), 32 * 1024 * 1024),
        name="ada",
    )(cp, ada_w, ada_b.reshape(depth, 1, n))
    return out[:, :bsz].reshape(depth, bsz, 9, d)


def _ffn_kernel(x_ref, m_ref, wa_ref, wb_ref, wo_ref, g_ref, b_ref, o_ref, *, sub):
    shift = m_ref[0, 3 * sub:3 * sub + 1, :]
    scale = m_ref[0, 3 * sub + 1:3 * sub + 2, :]
    gate = m_ref[0, 3 * sub + 2:3 * sub + 3, :]
    ts = x_ref.shape[0] // FFN_SPLIT
    rows = [slice(s * ts, (s + 1) * ts) for s in range(FFN_SPLIT)]
    hs = [(x_ref[r, :] * (1.0 + scale) + shift).astype(BF16) for r in rows]
    ups = [(jnp.dot(h, wa_ref[...], preferred_element_type=F32),
            jnp.dot(h, wb_ref[...], preferred_element_type=F32)) for h in hs]
    for r, (a, b) in zip(rows, ups):
        act = (_silu(a) * b).astype(BF16)
        y = jnp.dot(act, wo_ref[...], preferred_element_type=F32)
        z = ALPHA * x_ref[r, :] + (0.5 * gate) * y
        o_ref[r, :] = _layer_norm(z, g_ref[...], b_ref[...])


def _resident(shape, index_map):
    return pl.BlockSpec(shape, index_map, pipeline_mode=pl.Buffered(1))


def _ffn(x, mods, w_in, w_out, ln_g, ln_b, sub, seq):
    t, d = x.shape
    dff = w_out.shape[0]
    tm = min(FFN_TM, seq)
    per_b = seq // tm
    return pl.pallas_call(
        functools.partial(_ffn_kernel, sub=sub),
        grid=(t // tm,),
        in_specs=[
            pl.BlockSpec((tm, d), lambda i: (i, 0)),
            pl.BlockSpec((1, 9, d), lambda i: (i // per_b, 0, 0)),
            _resident((d, dff), lambda i: (0, 0)),
            _resident((d, dff), lambda i: (0, 1)),
            _resident((dff, d), lambda i: (0, 0)),
            _resident((1, d), lambda i: (0, 0)),
            _resident((1, d), lambda i: (0, 0)),
        ],
        out_specs=pl.BlockSpec((tm, d), lambda i: (i, 0)),
        out_shape=jax.ShapeDtypeStruct((t, d), F32),
        compiler_params=_params(("parallel",)),
        name="ffn",
    )(x, mods, w_in, w_in, w_out, ln_g.reshape(1, d), ln_b.reshape(1, d))


def _mm_kernel(x_ref, m_ref, w_ref, o_ref, h_scr):
    @pl.when(pl.program_id(1) == 0)
    def _():
        shift = m_ref[0, 3:4, :]
        scale = m_ref[0, 4:5, :]
        h_scr[...] = (x_ref[...] * (1.0 + scale) + shift).astype(BF16)

    o_ref[...] = jnp.dot(h_scr[...], w_ref[...], preferred_element_type=F32).astype(o_ref.dtype)


def _mod_matmul(x, mods, w, out_dtype, seq):
    t, d = x.shape
    n = w.shape[1]
    tm = min(MM_TM, seq)
    per_b = seq // tm
    return pl.pallas_call(
        _mm_kernel,
        grid=(t // tm, n // MM_TN),
        in_specs=[
            pl.BlockSpec((tm, d), lambda i, j: (i, 0)),
            pl.BlockSpec((1, 9, d), lambda i, j: (i // per_b, 0, 0)),
            pl.BlockSpec((d, MM_TN), lambda i, j: (0, j)),
        ],
        out_specs=pl.BlockSpec((tm, MM_TN), lambda i, j: (i, j)),
        out_shape=jax.ShapeDtypeStruct((t, n), out_dtype),
        scratch_shapes=[pltpu.VMEM((tm, d), BF16)],
        compiler_params=_params(("parallel", "arbitrary"), 40 * 1024 * 1024),
        name="mm",
    )(x, mods, w)


def _mixer_epilogue(x_ref, m_ref, y, g_ref, b_ref, o_ref):
    gate = m_ref[0, 5:6, :]
    z = ALPHA * x_ref[...] + gate * y
    o_ref[...] = _layer_norm(z, g_ref[...], b_ref[...])


def _out_na_kernel(a_ref, x_ref, m_ref, w_ref, g_ref, b_ref, o_ref):
    y = jnp.dot(a_ref[...], w_ref[...], preferred_element_type=F32)
    _mixer_epilogue(x_ref, m_ref, y, g_ref, b_ref, o_ref)


def _out_hgrn_kernel(of_ref, ob_ref, gt_ref, x_ref, m_ref, ng_ref, w_ref, g_ref, b_ref, o_ref):
    o = of_ref[...] + ob_ref[...]
    parts = []
    for h in range(HGRN_HEADS):
        oh = o[:, h * HGRN_HEAD_DIM:(h + 1) * HGRN_HEAD_DIM]
        ms = jnp.mean(oh * oh, axis=-1, keepdims=True)
        parts.append(oh * lax.rsqrt(ms + LN_EPS))
    on = jnp.concatenate(parts, axis=1)
    lhs = (on * ng_ref[...] * _silu(gt_ref[...])).astype(BF16)
    y = jnp.dot(lhs, w_ref[...], preferred_element_type=F32)
    _mixer_epilogue(x_ref, m_ref, y, g_ref, b_ref, o_ref)


def _out_s5_kernel(a_ref, x_ref, m_ref, w_ref, g_ref, b_ref, o_ref):
    d = x_ref.shape[1]
    yy = jnp.dot(a_ref[...], w_ref[...], preferred_element_type=F32)
    y = yy[:, :d] * jax.nn.sigmoid(yy[:, d:])
    _mixer_epilogue(x_ref, m_ref, y, g_ref, b_ref, o_ref)


def _out_call(kernel, name, lead_specs, lead_args, x, mods, mid_specs, mid_args, w, ln_g, ln_b, seq):
    t, d = x.shape
    tm = min(OUT_TM, seq)
    per_b = seq // tm
    in_specs = list(lead_specs) + [
        pl.BlockSpec((tm, d), lambda i: (i, 0)),
        pl.BlockSpec((1, 9, d), lambda i: (i // per_b, 0, 0)),
    ] + list(mid_specs) + [
        pl.BlockSpec(w.shape, lambda i: (0, 0)),
        pl.BlockSpec((1, d), lambda i: (0, 0)),
        pl.BlockSpec((1, d), lambda i: (0, 0)),
    ]
    return pl.pallas_call(
        kernel,
        grid=(t // tm,),
        in_specs=in_specs,
        out_specs=pl.BlockSpec((tm, d), lambda i: (i, 0)),
        out_shape=jax.ShapeDtypeStruct((t, d), F32),
        compiler_params=_params(("parallel",), 40 * 1024 * 1024),
        name=name,
    )(*lead_args, x, mods, *mid_args, w, ln_g.reshape(1, d), ln_b.reshape(1, d))


def _na_bias_table(rpb):
    qc = np.arange(GRID_W)[:, None]
    kc = np.arange(GRID_W)[None, :]
    cs = np.clip(qc - NA_COLS // 2, 0, GRID_W - NA_COLS)
    mask = (kc >= cs) & (kc < cs + NA_COLS)
    dc = np.clip(kc - qc + NA_COLS - 1, 0, 2 * NA_COLS - 2)
    tbl = jnp.where(jnp.asarray(mask)[None, None], rpb[:, :, jnp.asarray(dc)].astype(F32), -jnp.inf)
    tbl2 = jnp.concatenate([tbl[:, :-1], tbl[:, 1:]], axis=-1)
    return tbl2.reshape(NA_HEADS * (2 * NA_ROWS - 2), GRID_W, 2 * GRID_W)


def _na_kernel(q_ref, kp_ref, kc_ref, kn_ref, vp_ref, vc_ref, vn_ref, tbl_ref, o_ref, k_scr, v_scr, *, rows):
    w = GRID_W
    half = NA_RB * w // 2
    full = NA_RB * w
    k_scr[0:half] = kp_ref[...]
    k_scr[half:half + full] = kc_ref[...]
    k_scr[half + full:] = kn_ref[...]
    v_scr[0:half] = vp_ref[...]
    v_scr[half:half + full] = vc_ref[...]
    v_scr[half + full:] = vn_ref[...]

    r0 = pl.program_id(1) * NA_RB
    lane = lax.broadcasted_iota(jnp.int32, (w, 2 * NA_HEAD_DIM), 1)
    lo = lane < NA_HEAD_DIM
    nd = 2 * NA_ROWS - 2

    def row_body(rr, carry):
        r = r0 + rr
        rs = jnp.clip(r - NA_ROWS // 2, 0, rows - NA_ROWS)
        woff = pl.multiple_of((rs - r0 + NA_ROWS // 2) * w, w)
        rel = r - rs
        qoff = pl.multiple_of(rr * w, w)

        def pair_slice(h):
            return slice(2 * NA_HEAD_DIM * (h // 2), 2 * NA_HEAD_DIM * (h // 2 + 1))

        scores = []
        for h in range(NA_HEADS):
            sl = pair_slice(h)
            q2 = q_ref[pl.ds(qoff, w), sl]
            k2 = k_scr[pl.ds(woff, NA_ROWS * w), sl]
            msk = lo if h % 2 == 0 else jnp.logical_not(lo)
            qm = jnp.where(msk, q2, jnp.zeros_like(q2))
            scores.append(lax.dot_general(qm, k2, (((1,), (1,)), ((), ())), preferred_element_type=F32))
        probs, denom = [], []
        for h in range(NA_HEADS):
            bias = jnp.concatenate(
                [tbl_ref[h * nd + 2 * j + (NA_ROWS - 1) - rel] for j in range(NA_ROWS // 2)], axis=1)
            s = scores[h] + bias
            m = jnp.max(s, axis=1, keepdims=True)
            p = jnp.exp(s - m)
            denom.append(jnp.sum(p, axis=1, keepdims=True))
            probs.append(p.astype(BF16))
        res = []
        for h in range(NA_HEADS):
            v2 = v_scr[pl.ds(woff, NA_ROWS * w), pair_slice(h)]
            res.append(jnp.dot(probs[h], v2, preferred_element_type=F32) / denom[h])
        outs = [jnp.where(lo, res[2 * hp], res[2 * hp + 1]) for hp in range(NA_HEADS // 2)]
        o_ref[pl.ds(qoff, w), :] = jnp.concatenate(outs, axis=1).astype(o_ref.dtype)
        return carry

    lax.fori_loop(0, NA_RB, row_body, 0)


def _na_core(qkv, tbl, bsz, seq):
    t = qkv.shape[0]
    d = D_MODEL
    rows = seq // GRID_W
    nrb = rows // NA_RB
    full = NA_RB * GRID_W
    half = full // 2

    def cur(col):
        return pl.BlockSpec((full, d), lambda b, i: (b * nrb + i, col))

    def prev(col):
        return pl.BlockSpec((half, d), lambda b, i: (b * 2 * nrb + jnp.maximum(2 * i - 1, 0), col))

    def nxt(col):
        return pl.BlockSpec((half, d), lambda b, i: (b * 2 * nrb + jnp.minimum(2 * i + 2, 2 * nrb - 1), col))

    return pl.pallas_call(
        functools.partial(_na_kernel, rows=rows),
        grid=(bsz, nrb),
        in_specs=[cur(0), prev(1), cur(1), nxt(1), prev(2), cur(2), nxt(2),
                  pl.BlockSpec(tbl.shape, lambda b, i: (0, 0, 0))],
        out_specs=pl.BlockSpec((full, d), lambda b, i: (b * nrb + i, 0)),
        out_shape=jax.ShapeDtypeStruct((t, d), BF16),
        scratch_shapes=[pltpu.VMEM((2 * full, d), BF16), pltpu.VMEM((2 * full, d), BF16)],
        compiler_params=_params(("parallel", "arbitrary")),
        name="na",
    )(qkv, qkv, qkv, qkv, qkv, qkv, qkv, tbl)


def _gla_kernel(qf_ref, vf_ref, ff_ref, qb_ref, vb_ref, fb_ref, lb_ref, of_ref, ob_ref, st_scr):
    c = HGRN_CHUNK
    dh = HGRN_HEAD_DIM
    tdims = (((1,), (1,)), ((), ()))

    @pl.when(pl.program_id(1) == 0)
    def _():
        st_scr[...] = jnp.zeros_like(st_scr)

    row = lax.broadcasted_iota(jnp.int32, (c, c), 0)
    col = lax.broadcasted_iota(jnp.int32, (c, c), 1)
    tris = (col <= row, col >= row)
    tri_bf = [t.astype(F32).astype(BF16) for t in tris]
    mid_row = (c // 2, c // 2 - 1)
    last_row = (c - 1, 0)
    refs = ((qf_ref, vf_ref, ff_ref, of_ref), (qb_ref, vb_ref, fb_ref, ob_ref))

    def chunk_body(ci, carry):
        offs = (pl.multiple_of(ci * c, c), pl.multiple_of((GLA_CB - 1 - ci) * c, c))
        pre = []
        for dr in range(2):
            q_ref, v_ref, f_ref, _ = refs[dr]
            lbv = lb_ref[dr]
            f = lbv + (1.0 - lbv) * jax.nn.sigmoid(f_ref[pl.ds(offs[dr], c), :])
            g = jnp.log(f)
            k = 1.0 - f
            g1 = g.astype(BF16)
            r1 = g - g1.astype(F32)
            g2 = r1.astype(BF16)
            g3 = (r1 - g2.astype(F32)).astype(BF16)
            bc = (jnp.dot(tri_bf[dr], g1, preferred_element_type=F32)
                  + jnp.dot(tri_bf[dr], g2, preferred_element_type=F32)
                  + jnp.dot(tri_bf[dr], g3, preferred_element_type=F32))
            b_mid = bc[mid_row[dr]:mid_row[dr] + 1]
            b_last = bc[last_row[dr]:last_row[dr] + 1]
            qs = _silu(q_ref[pl.ds(offs[dr], c), :])
            pre.append(dict(
                qe=(qs * jnp.exp(bc - b_mid)).astype(BF16),
                ke=(k * jnp.exp(b_mid - bc)).astype(BF16),
                qd=(qs * jnp.exp(bc)).astype(BF16),
                kd=(k * jnp.exp(b_last - bc)).astype(BF16),
                dec=jnp.exp(b_last),
                v=v_ref[pl.ds(offs[dr], c), :].astype(BF16)))
        units = [(dr, h) for dr in range(2) for h in range(HGRN_HEADS)]
        sls = [slice(h * dh, (h + 1) * dh) for h in range(HGRN_HEADS)]
        att = [lax.dot_general(pre[dr]["qe"][:, sls[h]], pre[dr]["ke"][:, sls[h]], tdims,
                               preferred_element_type=F32) for dr, h in units]
        inter, upd = [], []
        for dr, h in units:
            st = st_scr[dr, sls[h], :]
            inter.append(lax.dot_general(pre[dr]["qd"][:, sls[h]], st.astype(BF16), tdims,
                                         preferred_element_type=F32))
            new = lax.dot_general(pre[dr]["v"][:, sls[h]], pre[dr]["kd"][:, sls[h]],
                                  (((0,), (0,)), ((), ())), preferred_element_type=F32)
            upd.append(pre[dr]["dec"][:, sls[h]] * st + new)
        for u, (dr, h) in enumerate(units):
            st_scr[dr, sls[h], :] = upd[u]
        for u, (dr, h) in enumerate(units):
            a = jnp.where(tris[dr], att[u], 0.0).astype(BF16)
            o = jnp.dot(a, pre[dr]["v"][:, sls[h]], preferred_element_type=F32) + inter[u]
            refs[dr][3][pl.ds(offs[dr], c), sls[h]] = o
        return carry

    lax.fori_loop(0, GLA_CB, chunk_body, 0)


def _gla(proj, lbs, bsz, seq):
    t = proj.shape[0]
    d = D_MODEL
    tb = GLA_CB * HGRN_CHUNK
    nblk = seq // tb

    def fwd(col):
        return pl.BlockSpec((tb, d), lambda b, i: (b * nblk + i, col))

    def bwd(col):
        return pl.BlockSpec((tb, d), lambda b, i: (b * nblk + nblk - 1 - i, col))

    return pl.pallas_call(
        _gla_kernel,
        grid=(bsz, nblk),
        in_specs=[fwd(0), fwd(1), fwd(2), bwd(0), bwd(1), bwd(3),
                  pl.BlockSpec((2, 1, d), lambda b, i: (0, 0, 0))],
        out_specs=[fwd(0), bwd(0)],
        out_shape=[jax.ShapeDtypeStruct((t, d), F32), jax.ShapeDtypeStruct((t, d), F32)],
        scratch_shapes=[pltpu.VMEM((2, d, HGRN_HEAD_DIM), F32)],
        compiler_params=_params(("parallel", "arbitrary"), 32 * 1024 * 1024),
        name="gla",
    )(proj, proj, proj, proj, proj, proj, lbs.reshape(2, 1, d))


def _s5_coef_kernel(a_ref, bt_ref, c_ref, ke_ref, wt_ref, lam_ref):
    L = S5_L
    ch = S5_GROUP_CH
    p = S5_STATE
    hi = lax.Precision.HIGHEST
    a = a_ref[0]
    cr = c_ref[0, 0]
    ci = c_ref[0, 1]

    zb, lam, ang_dt = [], [], []
    for dr in range(2):
        ar = a[3 * dr:3 * dr + 1]
        ai = a[3 * dr + 1:3 * dr + 2]
        dt = jnp.exp(a[3 * dr + 2:3 * dr + 3])
        mag = jnp.exp(ar * dt)
        ang = ai * dt
        lr = mag * jnp.cos(ang)
        li = mag * jnp.sin(ang)
        den = ar * ar + ai * ai
        zr = ((lr - 1.0) * ar + li * ai) / den
        zi = (li * ar - (lr - 1.0) * ai) / den
        br = bt_ref[0, 0]
        bi = bt_ref[0, 1]
        zb.append((zr * br - zi * bi, zr * bi + zi * br))
        ang_dt.append((ar * dt, ang))

    def powers(dr, m):
        lre, lang = ang_dt[dr]
        mg = jnp.exp(m * lre)
        return mg * jnp.cos(m * lang), mg * jnp.sin(m * lang)

    def outer(pw, coef):
        pr, pi = pw
        fr, fi = coef
        r = pr.shape[0]
        re = pr[:, None, :] * fr[None] - pi[:, None, :] * fi[None]
        im = pr[:, None, :] * fi[None] + pi[:, None, :] * fr[None]
        return re.reshape(r * ch, p), im.reshape(r * ch, p)

    n = lax.broadcasted_iota(jnp.int32, (2 * L, 1), 0)
    kseq = jnp.zeros((ch, 2 * L * ch), F32)
    for dr in range(2):
        lag = (n - (L - 1)) if dr == 0 else ((L - 1) - n)
        valid = (lag >= 0).astype(F32)
        pr, pi = powers(dr, jnp.maximum(lag, 0).astype(F32))
        gr, gi = outer((pr * valid, pi * valid), (cr, ci))
        gmat = jnp.concatenate([gr, gi], axis=1)
        zmat = jnp.concatenate([zb[dr][0], -zb[dr][1]], axis=1)
        kseq = kseq + lax.dot_general(zmat, gmat, (((1,), (1,)), ((), ())),
                                      precision=hi, preferred_element_type=F32)
    width = 2 * L * ch
    for s in range(L):
        off = (L - 1 - s) * ch
        win = pltpu.roll(kseq, (width - off) % width, 1)[:, :L * ch]
        ke_ref[0, s * ch:(s + 1) * ch, 0:L * ch] = win.astype(ke_ref.dtype)

    srow = lax.broadcasted_iota(jnp.int32, (L, 1), 0).astype(F32)
    e0 = outer(powers(0, (L - 1) - srow), zb[0])
    e1 = outer(powers(1, srow), zb[1])
    ke_ref[0, :, L * ch:] = jnp.concatenate(
        [e0[0], e0[1], e1[0], e1[1], e0[1], e0[0], e1[1], e1[0]], axis=1).astype(ke_ref.dtype)

    w0 = outer(powers(0, srow + 1.0), (cr, ci))
    w1 = outer(powers(1, L - srow), (cr, ci))
    wt_ref[0] = jnp.concatenate([w0[0], -w0[1], w1[0], -w1[1]], axis=1).astype(wt_ref.dtype)

    lf = powers(0, jnp.full((1, 1), float(L), F32))
    lb = powers(1, jnp.full((1, 1), float(L), F32))
    lam_ref[0] = jnp.concatenate([lf[0], lf[1], lb[0], lb[1]], axis=1)


def _s5_coefs(a_re, a_im, log_dt, b_re, b_im, c_re, c_im):
    g, p, ch = S5_GROUPS, S5_STATE, S5_GROUP_CH
    ldt = jnp.broadcast_to(log_dt[:, :, None], (2, g, p))
    zero = jnp.zeros((g, p), F32)
    amat = jnp.stack([a_re[0], a_im[0], ldt[0], a_re[1], a_im[1], ldt[1], zero, zero], axis=1)
    bt = jnp.stack([b_re, b_im], axis=1).transpose(0, 1, 3, 2)
    cc = jnp.stack([c_re, c_im], axis=1)
    return pl.pallas_call(
        _s5_coef_kernel,
        grid=(g,),
        in_specs=[
            pl.BlockSpec((1, 8, p), lambda i: (i, 0, 0)),
            pl.BlockSpec((1, 2, ch, p), lambda i: (i, 0, 0, 0)),
            pl.BlockSpec((1, 2, ch, p), lambda i: (i, 0, 0, 0)),
        ],
        out_specs=[
            pl.BlockSpec((1, S5_LC, S5_LC + 8 * p), lambda i: (i, 0, 0)),
            pl.BlockSpec((1, S5_LC, 4 * p), lambda i: (i, 0, 0)),
            pl.BlockSpec((1, 1, 4 * p), lambda i: (i, 0, 0)),
        ],
        out_shape=[
            jax.ShapeDtypeStruct((g, S5_LC, S5_LC + 8 * p), BF16),
            jax.ShapeDtypeStruct((g, S5_LC, 4 * p), BF16),
            jax.ShapeDtypeStruct((g, 1, 4 * p), F32),
        ],
        compiler_params=_params(("parallel",), 32 * 1024 * 1024),
        name="s5_coef",
    )(amat, bt, cc)


def _s5_in_kernel(x_ref, sc_ref, sh_ref, ke_ref, e_ref, *, zero_fill):
    p = S5_STATE
    b = pl.program_id(1)
    mb = x_ref.shape[1]
    if zero_fill:
        @pl.when(b == 0)
        def _():
            e_ref[...] = jnp.zeros_like(e_ref)
    u = x_ref[0] * (1.0 + sc_ref[0, 0]) + sh_ref[0, 0]
    res = jnp.dot(u.astype(BF16), ke_ref[0, :, S5_LC:], preferred_element_type=F32)
    for dr in range(2):
        rows = pl.ds(2 * b + dr, mb, stride=S5_SUB)
        e_ref[0, 0, rows, :] = res[:, 2 * p * dr:2 * p * (dr + 1)]
        e_ref[0, 1, rows, :] = res[:, 4 * p + 2 * p * dr:4 * p + 2 * p * (dr + 1)]


def _s5_scan_kernel(e_ref, lam_ref, s_ref):
    p = S5_STATE
    sub = S5_SUB
    gb = e_ref.shape[0]
    nch = e_ref.shape[2] // sub
    lam = lam_ref[...]
    lane = lax.broadcasted_iota(jnp.int32, (1, 1, 2 * p), 2)
    sign = jnp.where(lane < p, -1.0, 1.0)
    coef = []
    for dr in range(2):
        lr = lam[:, :, 2 * p * dr:2 * p * dr + p]
        li = lam[:, :, 2 * p * dr + p:2 * p * (dr + 1)]
        ca = jnp.broadcast_to(jnp.concatenate([lr, lr], axis=2), (gb, sub, 2 * p))
        cb = jnp.broadcast_to(jnp.concatenate([li, li], axis=2) * sign, (gb, sub, 2 * p))
        coef.append((ca, cb))

    def step(x, xs, e, es, cf):
        return cf[0] * x + cf[1] * xs + e, cf[0] * xs - cf[1] * x + es

    def body(j, carry):
        xf, xfs, xb, xbs = carry
        of = pl.multiple_of(j * sub, sub)
        ob = pl.multiple_of((nch - 1 - j) * sub, sub)
        s_ref[:, 0, pl.ds(of, sub), :] = xf
        s_ref[:, 1, pl.ds(ob, sub), :] = xb
        xf, xfs = step(xf, xfs, e_ref[:, 0, pl.ds(of, sub), :], e_ref[:, 1, pl.ds(of, sub), :], coef[0])
        xb, xbs = step(xb, xbs, e_ref[:, 0, pl.ds(ob, sub), :], e_ref[:, 1, pl.ds(ob, sub), :], coef[1])
        return xf, xfs, xb, xbs

    zero = jnp.zeros((gb, sub, 2 * p), F32)
    lax.fori_loop(0, nch, body, (zero, zero, zero, zero))


def _gelu_tanh(y):
    return 0.5 * y * (1.0 + jnp.tanh(math.sqrt(2.0 / math.pi) * (y + 0.044715 * (y * y * y))))


def _s5_out_kernel(x_ref, sc_ref, sh_ref, dk_ref, ke_ref, s_ref, wt_ref, y_ref):
    b = pl.program_id(1)
    mb = x_ref.shape[1]
    u = x_ref[0] * (1.0 + sc_ref[0, 0]) + sh_ref[0, 0]
    y = jnp.dot(u.astype(BF16), ke_ref[0, :, :S5_LC], preferred_element_type=F32)
    states = jnp.concatenate([s_ref[0, dr, pl.ds(2 * b + dr, mb, stride=S5_SUB), :] for dr in range(2)], axis=1)
    y = y + lax.dot_general(states.astype(BF16), wt_ref[0], (((1,), (1,)), ((), ())),
                            preferred_element_type=F32)
    y = y + dk_ref[0] * u
    y_ref[0] = _gelu_tanh(y).astype(y_ref.dtype)


def _s5_core(x, mods, coefs, d_skip, bsz, seq):
    ke, wt, lam = coefs
    t, d = x.shape
    g, ch, L, p = S5_GROUPS, S5_GROUP_CH, S5_L, S5_STATE
    lc = S5_LC
    m = t // L
    mb = seq // L
    gb = S5_SCAN_GROUPS
    assert 2 * bsz <= S5_SUB
    rows = S5_SUB * mb

    xt = x.reshape(m, L, g, ch).transpose(2, 0, 1, 3).reshape(g, m, lc)

    def tile_mod(v):
        v = v.reshape(bsz, g, 1, ch)
        return jnp.broadcast_to(v[:, :, None], (bsz, g, L, 1, ch)).reshape(bsz, g, 1, lc)

    shift_t = tile_mod(mods[:, 3])
    scale_t = tile_mod(mods[:, 4])
    dk_t = jnp.broadcast_to(d_skip.reshape(g, 1, 1, ch), (g, 1, L, ch)).reshape(g, 1, lc)

    x_spec = pl.BlockSpec((1, mb, lc), lambda gi, b: (gi, b, 0))
    mod_spec = pl.BlockSpec((1, 1, 1, lc), lambda gi, b: (b, gi, 0, 0))
    ke_spec = pl.BlockSpec((1, lc, lc + 8 * p), lambda gi, b: (gi, 0, 0))
    st_spec = pl.BlockSpec((1, 2, rows, 2 * p), lambda gi, b: (gi, 0, 0, 0))

    e = pl.pallas_call(
        functools.partial(_s5_in_kernel, zero_fill=2 * bsz < S5_SUB),
        grid=(g, bsz),
        in_specs=[x_spec, mod_spec, mod_spec, ke_spec],
        out_specs=st_spec,
        out_shape=jax.ShapeDtypeStruct((g, 2, rows, 2 * p), F32),
        compiler_params=_params(("parallel", "arbitrary"), 32 * 1024 * 1024),
        name="s5_in",
    )(xt, scale_t, shift_t, ke)

    states = pl.pallas_call(
        _s5_scan_kernel,
        grid=(g // gb,),
        in_specs=[
            pl.BlockSpec((gb, 2, rows, 2 * p), lambda gi: (gi, 0, 0, 0)),
            pl.BlockSpec((gb, 1, 4 * p), lambda gi: (gi, 0, 0)),
        ],
        out_specs=pl.BlockSpec((gb, 2, rows, 2 * p), lambda gi: (gi, 0, 0, 0)),
        out_shape=jax.ShapeDtypeStruct((g, 2, rows, 2 * p), F32),
        compiler_params=_params(("parallel",), 48 * 1024 * 1024),
        name="s5_scan",
    )(e, lam)

    yt = pl.pallas_call(
        _s5_out_kernel,
        grid=(g, bsz),
        in_specs=[
            x_spec, mod_spec, mod_spec,
            pl.BlockSpec((1, 1, lc), lambda gi, b: (gi, 0, 0)),
            ke_spec,
            st_spec,
            pl.BlockSpec((1, lc, 4 * p), lambda gi, b: (gi, 0, 0)),
        ],
        out_specs=pl.BlockSpec((1, mb, lc), lambda gi, b: (gi, b, 0)),
        out_shape=jax.ShapeDtypeStruct((g, m, lc), BF16),
        compiler_params=_params(("parallel", "parallel"), 32 * 1024 * 1024),
        name="s5_out",
    )(xt, scale_t, shift_t, dk_t, ke, states, wt)

    return yt.reshape(g, m, L, ch).transpose(1, 2, 0, 3).reshape(t, d)


def _row_spec(tm, d, col=0):
    return pl.BlockSpec((tm, d), lambda i: (i, col))


def _na_mixer(x, mods, w_qkv, rpb, w_out, ln_g, ln_b, bsz, seq):
    d = D_MODEL
    scale = jnp.concatenate([jnp.full((d,), NA_HEAD_DIM ** -0.5, F32), jnp.ones((2 * d,), F32)])
    qkv = _mod_matmul(x, mods, (w_qkv * scale).astype(BF16), BF16, seq)
    att = _na_core(qkv, _na_bias_table(rpb), bsz, seq)
    tm = min(OUT_TM, seq)
    return _out_call(_out_na_kernel, "out_na", [_row_spec(tm, d)], [att], x, mods, [], [],
                     w_out.astype(BF16), ln_g, ln_b, seq)


def _hgrn_mixer(x, mods, w_in, lbs, norm_g, w_out, ln_g, ln_b, bsz, seq):
    d = D_MODEL
    proj = _mod_matmul(x, mods, w_in.astype(BF16), F32, seq)
    o_f, o_b = _gla(proj, lbs, bsz, seq)
    tm = min(OUT_TM, seq)
    lead = [_row_spec(tm, d), _row_spec(tm, d), _row_spec(tm, d, 4)]
    return _out_call(_out_hgrn_kernel, "out_hgrn", lead, [o_f, o_b, proj], x, mods,
                     [pl.BlockSpec((1, d), lambda i: (0, 0))], [norm_g.reshape(1, d)],
                     w_out.astype(BF16), ln_g, ln_b, seq)


def _s5_mixer(x, mods, coefs, d_skip, w_out, ln_g, ln_b, bsz, seq):
    d = D_MODEL
    act = _s5_core(x, mods, coefs, d_skip, bsz, seq)
    tm = min(OUT_TM, seq)
    return _out_call(_out_s5_kernel, "out_s5", [_row_spec(tm, d)], [act], x, mods, [], [],
                     w_out.astype(BF16), ln_g, ln_b, seq)


def _hgrn_lower_bounds(lb_logits):
    p = jax.nn.softmax(lb_logits.astype(F32), axis=1)
    cs = jnp.cumsum(p, axis=1)
    return cs - cs[:, :1]


def kernel(x, c, ada_w, ada_b, ln_g, ln_b, ffn_w_in, ffn_w_out, na_w_qkv, na_rpb, na_w_out,
           hgrn_w_in, hgrn_lb_logits, hgrn_norm_g, hgrn_w_out,
           s5_a_re, s5_a_im, s5_log_dt, s5_b_re, s5_b_im, s5_c_re, s5_c_im, s5_d, s5_w_out):
    bsz, seq, d = x.shape
    depth = ada_w.shape[0]
    mods_all = _ada_mods(c, ada_w, ada_b)
    lbs = _hgrn_lower_bounds(hgrn_lb_logits)
    xf = x.reshape(bsz * seq, d)
    for i in range(depth):
        kind = i % N_MIXERS
        j = i // N_MIXERS
        mods = mods_all[i]
        xf = _ffn(xf, mods, ffn_w_in[i, 0].astype(BF16), ffn_w_out[i, 0].astype(BF16),
                  ln_g[i, 0], ln_b[i, 0], 0, seq)
        if kind == 0:
            xf = _na_mixer(xf, mods, na_w_qkv[j], na_rpb[j], na_w_out[j], ln_g[i, 1], ln_b[i, 1], bsz, seq)
        elif kind == 1:
            xf = _hgrn_mixer(xf, mods, hgrn_w_in[j], lbs[:, i], hgrn_norm_g[j], hgrn_w_out[j],
                             ln_g[i, 1], ln_b[i, 1], bsz, seq)
        else:
            coefs = _s5_coefs(s5_a_re[j], s5_a_im[j], s5_log_dt[j], s5_b_re[j], s5_b_im[j],
                              s5_c_re[j], s5_c_im[j])
            xf = _s5_mixer(xf, mods, coefs, s5_d[j], s5_w_out[j], ln_g[i, 1], ln_b[i, 1], bsz, seq)
        xf = _ffn(xf, mods, ffn_w_in[i, 1].astype(BF16), ffn_w_out[i, 1].astype(BF16),
                  ln_g[i, 2], ln_b[i, 2], 2, seq)
    return xf.reshape(bsz, seq, d)
```

```python
import functools
import math

import numpy as np
import jax
import jax.numpy as jnp
from jax import lax
from jax.experimental import pallas as pl
from jax.experimental.pallas import tpu as pltpu

F32 = jnp.float32
BF16 = jnp.bfloat16

D_MODEL = 1024
DEPTH = 4
N_MIXERS = 3
D_FF = 2816
LN_EPS = 1e-5
ALPHA = (2 * DEPTH) ** 0.25
GRID_W = 64
NA_HEADS = 16
NA_HEAD_DIM = D_MODEL // NA_HEADS
NA_ROWS = 8
NA_COLS = 16
HGRN_HEAD_DIM = 128
HGRN_HEADS = D_MODEL // HGRN_HEAD_DIM
HGRN_CHUNK = 64
S5_GROUP_CH = 16
S5_STATE = 64
S5_GROUPS = D_MODEL // S5_GROUP_CH

V7X_VMEM_BYTES = 64 * 1024 * 1024
VMEM_LIMIT = 56 * 1024 * 1024
FFN_TM = 512
FFN_SPLIT = 2
MM_TM = 512
MM_TN = 1024
OUT_TM = 512
ADA_TN = 1152
NA_RB = 8
GLA_CB = 4
S5_L = 16
S5_R = 8
S5_LB = 8


def _params(sem, limit=VMEM_LIMIT):
    return pltpu.CompilerParams(dimension_semantics=sem, vmem_limit_bytes=limit)


def _layer_norm(z, g, b):
    mu = jnp.mean(z, axis=-1, keepdims=True)
    zc = z - mu
    var = jnp.mean(zc * zc, axis=-1, keepdims=True)
    return zc * lax.rsqrt(var + LN_EPS) * g + b


def _silu(a):
    return a * jax.nn.sigmoid(a)


def _ada_kernel(c_ref, w_ref, b_ref, o_ref):
    c = c_ref[...]
    sc = _silu(c).astype(BF16)
    w = w_ref[0].astype(BF16)
    o_ref[0] = jnp.dot(sc, w, preferred_element_type=F32) + b_ref[0]


def _ada_mods(c, ada_w, ada_b):
    bsz = c.shape[0]
    depth, d, n = ada_w.shape
    rows = 8
    cp = jnp.zeros((rows, d), F32).at[:bsz].set(c)
    out = pl.pallas_call(
        _ada_kernel,
        grid=(depth, n // ADA_TN),
        in_specs=[
            pl.BlockSpec((rows, d), lambda l, j: (0, 0)),
            pl.BlockSpec((1, d, ADA_TN), lambda l, j: (l, 0, j)),
            pl.BlockSpec((1, 1, ADA_TN), lambda l, j: (l, 0, j)),
        ],
        out_specs=pl.BlockSpec((1, rows, ADA_TN), lambda l, j: (l, 0, j)),
        out_shape=jax.ShapeDtypeStruct((depth, rows, n), F32),
        compiler_params=_params(("arbitrary", "arbitrary"), 32 * 1024 * 1024),
        name="ada",
    )(cp, ada_w, ada_b.reshape(depth, 1, n))
    return out[:, :bsz].reshape(depth, bsz, 9, d)


def _ffn_kernel(x_ref, m_ref, wa_ref, wb_ref, wo_ref, g_ref, b_ref, o_ref, *, sub):
    shift = m_ref[0, 3 * sub:3 * sub + 1, :]
    scale = m_ref[0, 3 * sub + 1:3 * sub + 2, :]
    gate = m_ref[0, 3 * sub + 2:3 * sub + 3, :]
    ts = x_ref.shape[0] // FFN_SPLIT
    rows = [slice(s * ts, (s + 1) * ts) for s in range(FFN_SPLIT)]
    hs = [(x_ref[r, :] * (1.0 + scale) + shift).astype(BF16) for r in rows]
    ups = [(jnp.dot(h, wa_ref[...], preferred_element_type=F32),
            jnp.dot(h, wb_ref[...], preferred_element_type=F32)) for h in hs]
    for r, (a, b) in zip(rows, ups):
        act = (_silu(a) * b).astype(BF16)
        y = jnp.dot(act, wo_ref[...], preferred_element_type=F32)
        z = ALPHA * x_ref[r, :] + (0.5 * gate) * y
        o_ref[r, :] = _layer_norm(z, g_ref[...], b_ref[...])


def _resident(shape, index_map):
    return pl.BlockSpec(shape, index_map, pipeline_mode=pl.Buffered(1))


def _ffn(x, mods, w_in, w_out, layer, half, ln_g, ln_b, sub, seq):
    t, d = x.shape
    dff = w_out.shape[2]
    tm = min(FFN_TM, seq)
    per_b = seq // tm
    return pl.pallas_call(
        functools.partial(_ffn_kernel, sub=sub),
        grid=(t // tm,),
        in_specs=[
            pl.BlockSpec((tm, d), lambda i: (i, 0)),
            pl.BlockSpec((1, 9, d), lambda i: (i // per_b, 0, 0)),
            _resident((None, None, d, dff), lambda i: (layer, half, 0, 0)),
            _resident((None, None, d, dff), lambda i: (layer, half, 0, 1)),
            _resident((None, None, dff, d), lambda i: (layer, half, 0, 0)),
            _resident((1, d), lambda i: (0, 0)),
            _resident((1, d), lambda i: (0, 0)),
        ],
        out_specs=pl.BlockSpec((tm, d), lambda i: (i, 0)),
        out_shape=jax.ShapeDtypeStruct((t, d), F32),
        compiler_params=_params(("parallel",)),
        name="ffn",
    )(x, mods, w_in, w_in, w_out, ln_g.reshape(1, d), ln_b.reshape(1, d))


def _mm_kernel(x_ref, m_ref, w_ref, o_ref):
    shift = m_ref[0, 3:4, :]
    scale = m_ref[0, 4:5, :]
    h = (x_ref[...] * (1.0 + scale) + shift).astype(BF16)
    for j in range(w_ref.shape[1] // MM_TN):
        cols = slice(j * MM_TN, (j + 1) * MM_TN)
        o_ref[:, cols] = jnp.dot(h, w_ref[:, cols], preferred_element_type=F32).astype(o_ref.dtype)


def _mod_matmul(x, mods, w, out_dtype, seq):
    t, d = x.shape
    n = w.shape[1]
    tm = min(MM_TM, seq)
    per_b = seq // tm
    return pl.pallas_call(
        _mm_kernel,
        grid=(t // tm,),
        in_specs=[
            pl.BlockSpec((tm, d), lambda i: (i, 0)),
            pl.BlockSpec((1, 9, d), lambda i: (i // per_b, 0, 0)),
            _resident((d, n), lambda i: (0, 0)),
        ],
        out_specs=pl.BlockSpec((tm, n), lambda i: (i, 0)),
        out_shape=jax.ShapeDtypeStruct((t, n), out_dtype),
        compiler_params=_params(("parallel",), 48 * 1024 * 1024),
        name="mm",
    )(x, mods, w)


def _mixer_epilogue(x_ref, m_ref, y, g_ref, b_ref, o_ref):
    gate = m_ref[0, 5:6, :]
    z = ALPHA * x_ref[...] + gate * y
    o_ref[...] = _layer_norm(z, g_ref[...], b_ref[...])


def _out_hgrn_kernel(of_ref, ob_ref, gt_ref, x_ref, m_ref, ng_ref, w_ref, g_ref, b_ref, o_ref):
    o = of_ref[...] + ob_ref[...]
    parts = []
    for h in range(HGRN_HEADS):
        oh = o[:, h * HGRN_HEAD_DIM:(h + 1) * HGRN_HEAD_DIM]
        ms = jnp.mean(oh * oh, axis=-1, keepdims=True)
        parts.append(oh * lax.rsqrt(ms + LN_EPS))
    on = jnp.concatenate(parts, axis=1)
    lhs = (on * ng_ref[...] * _silu(gt_ref[...])).astype(BF16)
    y = jnp.dot(lhs, w_ref[...], preferred_element_type=F32)
    _mixer_epilogue(x_ref, m_ref, y, g_ref, b_ref, o_ref)


def _out_s5_kernel(a_ref, x_ref, m_ref, w_ref, g_ref, b_ref, o_ref):
    d = x_ref.shape[1]
    yy = jnp.dot(a_ref[...].astype(BF16), w_ref[...], preferred_element_type=F32)
    y = yy[:, :d] * jax.nn.sigmoid(yy[:, d:])
    _mixer_epilogue(x_ref, m_ref, y, g_ref, b_ref, o_ref)


def _out_call(kernel, name, lead_specs, lead_args, x, mods, mid_specs, mid_args, w, ln_g, ln_b, seq):
    t, d = x.shape
    tm = min(OUT_TM, seq)
    per_b = seq // tm
    in_specs = list(lead_specs) + [
        pl.BlockSpec((tm, d), lambda i: (i, 0)),
        pl.BlockSpec((1, 9, d), lambda i: (i // per_b, 0, 0)),
    ] + list(mid_specs) + [
        _resident(w.shape, lambda i: (0, 0)),
        _resident((1, d), lambda i: (0, 0)),
        _resident((1, d), lambda i: (0, 0)),
    ]
    return pl.pallas_call(
        kernel,
        grid=(t // tm,),
        in_specs=in_specs,
        out_specs=pl.BlockSpec((tm, d), lambda i: (i, 0)),
        out_shape=jax.ShapeDtypeStruct((t, d), F32),
        compiler_params=_params(("parallel",), 40 * 1024 * 1024),
        name=name,
    )(*lead_args, x, mods, *mid_args, w, ln_g.reshape(1, d), ln_b.reshape(1, d))


def _na_bias_table(rpb):
    qc = np.arange(GRID_W)[:, None]
    kc = np.arange(GRID_W)[None, :]
    cs = np.clip(qc - NA_COLS // 2, 0, GRID_W - NA_COLS)
    mask = (kc >= cs) & (kc < cs + NA_COLS)
    dc = np.clip(kc - qc + NA_COLS - 1, 0, 2 * NA_COLS - 2)
    onehot = (dc.reshape(1, -1) == np.arange(2 * NA_COLS - 1)[:, None]).astype(np.float32)
    cols = jnp.dot(rpb.astype(F32).reshape(-1, 2 * NA_COLS - 1), jnp.asarray(onehot),
                   precision=lax.Precision.HIGHEST)
    cols = cols.reshape(NA_HEADS, 2 * NA_ROWS - 1, GRID_W, GRID_W)
    tbl = jnp.where(jnp.asarray(mask)[None, None], cols, -jnp.inf)
    tbl2 = jnp.concatenate([tbl[:, :-1], tbl[:, 1:]], axis=-1)
    return tbl2.reshape(NA_HEADS * (2 * NA_ROWS - 2), GRID_W, 2 * GRID_W)


def _na_kernel(q_ref, kp_ref, kc_ref, kn_ref, vp_ref, vc_ref, vn_ref, tbl_ref,
               x_ref, m_ref, wo_ref, g_ref, b_ref, o_ref, k_scr, v_scr, att_scr, *, rows):
    w = GRID_W
    half = NA_RB * w // 2
    full = NA_RB * w
    k_scr[0:half] = kp_ref[...]
    k_scr[half:half + full] = kc_ref[...]
    k_scr[half + full:] = kn_ref[...]
    v_scr[0:half] = vp_ref[...]
    v_scr[half:half + full] = vc_ref[...]
    v_scr[half + full:] = vn_ref[...]

    r0 = pl.program_id(1) * NA_RB
    lane = lax.broadcasted_iota(jnp.int32, (w, 2 * NA_HEAD_DIM), 1)
    lo = lane < NA_HEAD_DIM
    nd = 2 * NA_ROWS - 2

    def row_body(rr, carry):
        r = r0 + rr
        rs = jnp.clip(r - NA_ROWS // 2, 0, rows - NA_ROWS)
        woff = pl.multiple_of((rs - r0 + NA_ROWS // 2) * w, w)
        rel = r - rs
        qoff = pl.multiple_of(rr * w, w)

        def pair_slice(h):
            return slice(2 * NA_HEAD_DIM * (h // 2), 2 * NA_HEAD_DIM * (h // 2 + 1))

        scores = []
        for h in range(NA_HEADS):
            sl = pair_slice(h)
            q2 = q_ref[pl.ds(qoff, w), sl]
            k2 = k_scr[pl.ds(woff, NA_ROWS * w), sl]
            msk = lo if h % 2 == 0 else jnp.logical_not(lo)
            qm = jnp.where(msk, q2, jnp.zeros_like(q2))
            scores.append(lax.dot_general(qm, k2, (((1,), (1,)), ((), ())), preferred_element_type=F32))
        probs, denom = [], []
        for h in range(NA_HEADS):
            bias = jnp.concatenate(
                [tbl_ref[h * nd + 2 * j + (NA_ROWS - 1) - rel] for j in range(NA_ROWS // 2)], axis=1)
            s = scores[h] + bias
            m = jnp.max(s, axis=1, keepdims=True)
            p = jnp.exp(s - m)
            denom.append(jnp.sum(p, axis=1, keepdims=True))
            probs.append(p.astype(BF16))
        res = []
        for h in range(NA_HEADS):
            v2 = v_scr[pl.ds(woff, NA_ROWS * w), pair_slice(h)]
            res.append(jnp.dot(probs[h], v2, preferred_element_type=F32) / denom[h])
        outs = [jnp.where(lo, res[2 * hp], res[2 * hp + 1]) for hp in range(NA_HEADS // 2)]
        att_scr[pl.ds(qoff, w), :] = jnp.concatenate(outs, axis=1).astype(att_scr.dtype)
        return carry

    lax.fori_loop(0, NA_RB, row_body, 0)
    y = jnp.dot(att_scr[...], wo_ref[...], preferred_element_type=F32)
    _mixer_epilogue(x_ref, m_ref, y, g_ref, b_ref, o_ref)


def _na_core(qkv, tbl, x, mods, w_out, ln_g, ln_b, bsz, seq):
    t = qkv.shape[0]
    d = D_MODEL
    rows = seq // GRID_W
    nrb = rows // NA_RB
    full = NA_RB * GRID_W
    half = full // 2

    def cur(col):
        return pl.BlockSpec((full, d), lambda b, i: (b * nrb + i, col))

    def prev(col):
        return pl.BlockSpec((half, d), lambda b, i: (b * 2 * nrb + jnp.maximum(2 * i - 1, 0), col))

    def nxt(col):
        return pl.BlockSpec((half, d), lambda b, i: (b * 2 * nrb + jnp.minimum(2 * i + 2, 2 * nrb - 1), col))

    return pl.pallas_call(
        functools.partial(_na_kernel, rows=rows),
        grid=(bsz, nrb),
        in_specs=[cur(0), prev(1), cur(1), nxt(1), prev(2), cur(2), nxt(2),
                  _resident(tbl.shape, lambda b, i: (0, 0, 0)),
                  cur(0),
                  pl.BlockSpec((1, 9, d), lambda b, i: (b, 0, 0)),
                  _resident((d, d), lambda b, i: (0, 0)),
                  _resident((1, d), lambda b, i: (0, 0)),
                  _resident((1, d), lambda b, i: (0, 0))],
        out_specs=pl.BlockSpec((full, d), lambda b, i: (b * nrb + i, 0)),
        out_shape=jax.ShapeDtypeStruct((t, d), F32),
        scratch_shapes=[pltpu.VMEM((2 * full, d), BF16), pltpu.VMEM((2 * full, d), BF16),
                        pltpu.VMEM((full, d), BF16)],
        compiler_params=_params(("parallel", "arbitrary")),
        name="na",
    )(qkv, qkv, qkv, qkv, qkv, qkv, qkv, tbl, x, mods, w_out, ln_g.reshape(1, d), ln_b.reshape(1, d))


def _gla_kernel(qf_ref, vf_ref, ff_ref, qb_ref, vb_ref, fb_ref, lb_ref, of_ref, ob_ref, st_scr):
    c = HGRN_CHUNK
    dh = HGRN_HEAD_DIM
    tdims = (((1,), (1,)), ((), ()))

    @pl.when(pl.program_id(1) == 0)
    def _():
        st_scr[...] = jnp.zeros_like(st_scr)

    row = lax.broadcasted_iota(jnp.int32, (c, c), 0)
    col = lax.broadcasted_iota(jnp.int32, (c, c), 1)
    tris = (col <= row, col >= row)
    tri3 = [jnp.concatenate([t.astype(F32).astype(BF16)] * 3, axis=1) for t in tris]
    mid_row = (c // 2, c // 2 - 1)
    last_row = (c - 1, 0)
    refs = ((qf_ref, vf_ref, ff_ref, of_ref), (qb_ref, vb_ref, fb_ref, ob_ref))

    def chunk_body(ci, carry):
        offs = (pl.multiple_of(ci * c, c), pl.multiple_of((GLA_CB - 1 - ci) * c, c))
        pre = []
        for dr in range(2):
            q_ref, v_ref, f_ref, _ = refs[dr]
            lbv = lb_ref[dr]
            f = lbv + (1.0 - lbv) * jax.nn.sigmoid(f_ref[pl.ds(offs[dr], c), :])
            g = jnp.log(f)
            k = 1.0 - f
            g1 = g.astype(BF16)
            r1 = g - g1.astype(F32)
            g2 = r1.astype(BF16)
            g3 = (r1 - g2.astype(F32)).astype(BF16)
            bc = jnp.dot(tri3[dr], jnp.concatenate([g1, g2, g3], axis=0), preferred_element_type=F32)
            b_mid = bc[mid_row[dr]:mid_row[dr] + 1]
            b_last = bc[last_row[dr]:last_row[dr] + 1]
            qs = _silu(q_ref[pl.ds(offs[dr], c), :])
            pre.append(dict(
                qe=(qs * jnp.exp(bc - b_mid)).astype(BF16),
                ke=(k * jnp.exp(b_mid - bc)).astype(BF16),
                qd=(qs * jnp.exp(bc)).astype(BF16),
                kd=(k * jnp.exp(b_last - bc)).astype(BF16),
                dec=jnp.exp(b_last),
                v=v_ref[pl.ds(offs[dr], c), :].astype(BF16)))
        units = [(dr, h) for dr in range(2) for h in range(HGRN_HEADS)]
        sls = [slice(h * dh, (h + 1) * dh) for h in range(HGRN_HEADS)]
        att = [lax.dot_general(pre[dr]["qe"][:, sls[h]], pre[dr]["ke"][:, sls[h]], tdims,
                               preferred_element_type=F32) for dr, h in units]
        inter, upd = [], []
        for dr, h in units:
            st = st_scr[dr, sls[h], :]
            inter.append(lax.dot_general(pre[dr]["qd"][:, sls[h]], st.astype(BF16), tdims,
                                         preferred_element_type=F32))
            new = lax.dot_general(pre[dr]["v"][:, sls[h]], pre[dr]["kd"][:, sls[h]],
                                  (((0,), (0,)), ((), ())), preferred_element_type=F32)
            upd.append(pre[dr]["dec"][:, sls[h]] * st + new)
        for u, (dr, h) in enumerate(units):
            st_scr[dr, sls[h], :] = upd[u]
        for u, (dr, h) in enumerate(units):
            a = jnp.where(tris[dr], att[u], 0.0).astype(BF16)
            o = jnp.dot(a, pre[dr]["v"][:, sls[h]], preferred_element_type=F32) + inter[u]
            refs[dr][3][pl.ds(offs[dr], c), sls[h]] = o
        return carry

    lax.fori_loop(0, GLA_CB, chunk_body, 0)


def _gla(proj, lbs, bsz, seq):
    t = proj.shape[0]
    d = D_MODEL
    tb = GLA_CB * HGRN_CHUNK
    nblk = seq // tb

    def fwd(col):
        return pl.BlockSpec((tb, d), lambda b, i: (b * nblk + i, col))

    def bwd(col):
        return pl.BlockSpec((tb, d), lambda b, i: (b * nblk + nblk - 1 - i, col))

    return pl.pallas_call(
        _gla_kernel,
        grid=(bsz, nblk),
        in_specs=[fwd(0), fwd(1), fwd(2), bwd(0), bwd(1), bwd(3),
                  pl.BlockSpec((2, 1, d), lambda b, i: (0, 0, 0))],
        out_specs=[fwd(0), bwd(0)],
        out_shape=[jax.ShapeDtypeStruct((t, d), F32), jax.ShapeDtypeStruct((t, d), F32)],
        scratch_shapes=[pltpu.VMEM((2, d, HGRN_HEAD_DIM), F32)],
        compiler_params=_params(("parallel", "arbitrary"), 32 * 1024 * 1024),
        name="gla",
    )(proj, proj, proj, proj, proj, proj, lbs.reshape(2, 1, d))


def _gelu_tanh(y):
    return 0.5 * y * (1.0 + jnp.tanh(math.sqrt(2.0 / math.pi) * (y + 0.044715 * (y * y * y))))


def _s5n_coef_kernel(a_ref, bt_ref, c_ref, k2_ref, e8_ref, w8_ref, sc_ref, lam_ref):
    L, ch, p, R = S5_L, S5_GROUP_CH, S5_STATE, S5_R
    lanes = 2 * p
    hi = lax.Precision.HIGHEST
    e8_ref[...] = jnp.zeros_like(e8_ref)
    w8_ref[...] = jnp.zeros_like(w8_ref)
    for g8 in range(S5_LB):
        a = a_ref[g8]
        cr, ci = c_ref[g8, 0], c_ref[g8, 1]
        br, bi = bt_ref[g8, 0], bt_ref[g8, 1]
        zb, ang_dt = [], []
        for dr in range(2):
            ar = a[3 * dr:3 * dr + 1]
            ai = a[3 * dr + 1:3 * dr + 2]
            dt = jnp.exp(a[3 * dr + 2:3 * dr + 3])
            mag = jnp.exp(ar * dt)
            ang = ai * dt
            lr = mag * jnp.cos(ang)
            li = mag * jnp.sin(ang)
            den = ar * ar + ai * ai
            zr = ((lr - 1.0) * ar + li * ai) / den
            zi = (li * ar - (lr - 1.0) * ai) / den
            zb.append((zr * br - zi * bi, zr * bi + zi * br))
            ang_dt.append((ar * dt, ang))

        def powers(dr, m):
            lre, lang = ang_dt[dr]
            mg = jnp.exp(m * lre)
            return mg * jnp.cos(m * lang), mg * jnp.sin(m * lang)

        def outer(pw, coef, sign=1.0):
            pr, pi = pw
            fr, fi = coef
            r = pr.shape[0]
            re = pr[:, None, :] * fr[None] - pi[:, None, :] * fi[None]
            im = pr[:, None, :] * fi[None] + pi[:, None, :] * fr[None]
            return jnp.concatenate([re.reshape(r * ch, p), sign * im.reshape(r * ch, p)], axis=1)

        rows = slice(g8 * ch, (g8 + 1) * ch)
        n = lax.broadcasted_iota(jnp.int32, (2 * L, 1), 0)
        ccp = [jnp.zeros((g8 * ch, lanes), F32), jnp.concatenate([cr, -ci], axis=1),
               jnp.zeros(((S5_LB - 1 - g8) * ch, lanes), F32)]
        ccp = jnp.concatenate([v for v in ccp if v.shape[0]], axis=0)
        kn = jnp.zeros((2 * L * ch, S5_LB * ch), F32)
        for dr in range(2):
            lag = ((L - 1) - n) if dr == 0 else (n - (L - 1))
            valid = jnp.logical_and(lag >= 0, n <= 2 * L - 2).astype(F32)
            pr, pi = powers(dr, jnp.maximum(lag, 0).astype(F32))
            zbp = outer((pr * valid, pi * valid), zb[dr])
            kn = kn + lax.dot_general(zbp, ccp, (((1,), (1,)), ((), ())), precision=hi,
                                      preferred_element_type=F32)
        kn3 = kn.reshape(2 * L, ch, S5_LB * ch).astype(k2_ref.dtype)
        half = S5_LB * ch
        k2_ref[0, :, rows, 0:half] = kn3
        k2_ref[0, 1:, rows, half:] = kn3[:-1]
        k2_ref[0, 0:1, rows, half:] = jnp.zeros((1, ch, half), k2_ref.dtype)

        rr = lax.broadcasted_iota(jnp.int32, (R, 1), 0).astype(F32)
        pieces = (
            (e8_ref, outer(powers(0, (R - 1) - rr), zb[0]), outer(powers(1, rr), zb[1])),
            (w8_ref, outer(powers(0, rr + 1.0), (cr, ci), -1.0), outer(powers(1, R - rr), (cr, ci), -1.0)),
        )
        for ref, fwd, bwd in pieces:
            for dr, piece in enumerate((fwd, bwd)):
                col = (2 * g8 + dr) * lanes
                ref[0, :, rows, col:col + lanes] = piece.reshape(R, ch, lanes).astype(ref.dtype)

        lane = lax.broadcasted_iota(jnp.int32, (1, lanes), 1)
        sign = jnp.where(lane < p, -1.0, 1.0)

        def mul_rows(dr, m):
            pr, pi = powers(dr, jnp.full((1, 1), float(m), F32))
            return jnp.concatenate([pr, pr], axis=1), jnp.concatenate([pi, pi], axis=1) * sign

        na = L // R
        for a_i in range(na):
            for dr, m in ((0, R * a_i), (1, R * (na - 1 - a_i))):
                col = (2 * g8 + dr) * lanes
                ca, cb = mul_rows(dr, m)
                sc_ref[0, a_i:a_i + 1, col:col + lanes] = ca
                sc_ref[0, na + a_i:na + a_i + 1, col:col + lanes] = cb
        for dr in range(2):
            ca, cb = mul_rows(dr, L)
            lam_ref[0, 2 * dr, g8:g8 + 1, :] = ca
            lam_ref[0, 2 * dr + 1, g8:g8 + 1, :] = cb


def _s5n_coefs(a_re, a_im, log_dt, b_re, b_im, c_re, c_im):
    g, p, ch, L, R = S5_GROUPS, S5_STATE, S5_GROUP_CH, S5_L, S5_R
    nlb = g // S5_LB
    lw = S5_LB * ch
    sw = S5_LB * 4 * p
    ldt = jnp.broadcast_to(log_dt[:, :, None], (2, g, p))
    zero = jnp.zeros((g, p), F32)
    amat = jnp.stack([a_re[0], a_im[0], ldt[0], a_re[1], a_im[1], ldt[1], zero, zero], axis=1)
    bt = jnp.stack([b_re, b_im], axis=1).transpose(0, 1, 3, 2)
    cc = jnp.stack([c_re, c_im], axis=1)
    k2, e8, w8, sc, lam = pl.pallas_call(
        _s5n_coef_kernel,
        grid=(nlb,),
        in_specs=[
            pl.BlockSpec((S5_LB, 8, p), lambda i: (i, 0, 0)),
            pl.BlockSpec((S5_LB, 2, ch, p), lambda i: (i, 0, 0, 0)),
            pl.BlockSpec((S5_LB, 2, ch, p), lambda i: (i, 0, 0, 0)),
        ],
        out_specs=[
            pl.BlockSpec((1, 2 * L, lw, 2 * lw), lambda i: (i, 0, 0, 0)),
            pl.BlockSpec((1, R, lw, sw), lambda i: (i, 0, 0, 0)),
            pl.BlockSpec((1, R, lw, sw), lambda i: (i, 0, 0, 0)),
            pl.BlockSpec((1, 2 * (L // R), sw), lambda i: (i, 0, 0)),
            pl.BlockSpec((1, 4, S5_LB, 2 * p), lambda i: (i, 0, 0, 0)),
        ],
        out_shape=[
            jax.ShapeDtypeStruct((nlb, 2 * L, lw, 2 * lw), BF16),
            jax.ShapeDtypeStruct((nlb, R, lw, sw), BF16),
            jax.ShapeDtypeStruct((nlb, R, lw, sw), BF16),
            jax.ShapeDtypeStruct((nlb, 2 * (L // R), sw), F32),
            jax.ShapeDtypeStruct((nlb, 4, S5_LB, 2 * p), F32),
        ],
        compiler_params=_params(("parallel",), 48 * 1024 * 1024),
        name="s5_coef",
    )(amat, bt, cc)
    return k2.reshape(nlb, 2 * L * lw, 2 * lw), e8.reshape(nlb, R * lw, sw), w8.reshape(nlb, R * lw, sw), sc, lam


def _swap_halves(v):
    w = 2 * S5_STATE
    return jnp.concatenate([pltpu.roll(v[:, k * w:(k + 1) * w], S5_STATE, 1) for k in range(v.shape[1] // w)],
                           axis=1)


def _s5n_load_chunks(x_ref, m_ref, s):
    tm = x_ref.shape[0] // S5_L
    return x_ref[pl.ds(s, tm, stride=S5_L), :] * (1.0 + m_ref[0, 4:5, :]) + m_ref[0, 3:4, :]


def _s5n_in_kernel(x_ref, m_ref, e8_ref, sc_ref, e_ref):
    L, R, w = S5_L, S5_R, 2 * S5_STATE
    na = L // R
    tm = x_ref.shape[0] // L
    acc = jnp.zeros((tm, e8_ref.shape[2]), F32)
    for a in range(na):
        xa = jnp.concatenate([_s5n_load_chunks(x_ref, m_ref, R * a + r).astype(BF16) for r in range(R)], axis=1)
        ea = jnp.dot(xa, e8_ref[0], preferred_element_type=F32)
        ta = na - 1 - a
        acc = acc + sc_ref[0, ta:ta + 1, :] * ea + sc_ref[0, na + ta:na + ta + 1, :] * _swap_halves(ea)
    for plane, val in enumerate((acc, _swap_halves(acc))):
        for g8 in range(S5_LB):
            for dr in range(2):
                col = (2 * g8 + dr) * w
                e_ref[0, dr, plane, 0, pl.ds(g8, tm, stride=S5_LB), :] = val[:, col:col + w]


def _s5n_scan_kernel(e_ref, lam_ref, s_ref):
    nb = e_ref.shape[3]
    sub = S5_LB
    nch = e_ref.shape[4] // sub
    dr = pl.program_id(1)
    ca, cb = lam_ref[0, 0], lam_ref[0, 1]

    def body(j, carry):
        off = pl.multiple_of(jnp.where(dr == 0, j, nch - 1 - j) * sub, sub)
        new = []
        for b in range(nb):
            x, xs = carry[2 * b], carry[2 * b + 1]
            s_ref[0, 0, b, pl.ds(off, sub), :] = x
            new.append(ca * x + cb * xs + e_ref[0, 0, 0, b, pl.ds(off, sub), :])
            new.append(ca * xs - cb * x + e_ref[0, 0, 1, b, pl.ds(off, sub), :])
        return tuple(new)

    zero = jnp.zeros((sub, 2 * S5_STATE), F32)
    lax.fori_loop(0, nch, body, tuple([zero] * (2 * nb)))


def _s5n_out_kernel(x_ref, m_ref, d_ref, k2_ref, s_ref, w8_ref, sc_ref, y_ref):
    L, R, w = S5_L, S5_R, 2 * S5_STATE
    na = L // R
    lw = x_ref.shape[1]
    tm = x_ref.shape[0] // L
    xs = [_s5n_load_chunks(x_ref, m_ref, s) for s in range(L)]
    xflat = jnp.concatenate([v.astype(BF16) for v in xs], axis=1)
    st = jnp.concatenate([s_ref[0, dr, 0, pl.ds(g8, tm, stride=S5_LB), :]
                          for g8 in range(S5_LB) for dr in range(2)], axis=1)
    stsw = _swap_halves(st)
    corr = []
    for a in range(na):
        sa = (sc_ref[0, a:a + 1, :] * st + sc_ref[0, na + a:na + a + 1, :] * stsw).astype(BF16)
        corr.append(lax.dot_general(sa, w8_ref[0], (((1,), (1,)), ((), ())), preferred_element_type=F32))
    for tp in range(L // 2):
        start = (L - 1 - 2 * tp) * lw
        yp = jnp.dot(xflat, k2_ref[0, start:start + L * lw, :], preferred_element_type=F32)
        for q in range(2):
            t = 2 * tp + q
            y = yp[:, q * lw:(q + 1) * lw] + corr[t // R][:, (t % R) * lw:(t % R + 1) * lw] + d_ref[...] * xs[t]
            y_ref[pl.ds(t, tm, stride=L), :] = _gelu_tanh(y)


def _s5n_core(x, mods, coefs, d_skip, bsz, seq):
    k2, e8, w8, sc, lam = coefs
    t, d = x.shape
    L, p = S5_L, S5_STATE
    lw = S5_LB * S5_GROUP_CH
    nlb = d // lw
    mb = seq // L
    rows = mb * S5_LB

    x_spec = pl.BlockSpec((seq, lw), lambda gi, b: (b, gi))
    m_spec = pl.BlockSpec((1, 9, lw), lambda gi, b: (b, 0, gi))
    st_spec = pl.BlockSpec((1, 2, 1, rows, 2 * p), lambda gi, b: (gi, 0, b, 0, 0))
    st_shape = jax.ShapeDtypeStruct((nlb, 2, bsz, rows, 2 * p), F32)

    def per_block(arr):
        return pl.BlockSpec((1,) + arr.shape[1:], lambda gi, b: (gi,) + (0,) * (arr.ndim - 1),
                            pipeline_mode=pl.Buffered(1))

    e = pl.pallas_call(
        _s5n_in_kernel,
        grid=(nlb, bsz),
        in_specs=[x_spec, m_spec, per_block(e8), per_block(sc)],
        out_specs=pl.BlockSpec((1, 2, 2, 1, rows, 2 * p), lambda gi, b: (gi, 0, 0, b, 0, 0)),
        out_shape=jax.ShapeDtypeStruct((nlb, 2, 2, bsz, rows, 2 * p), F32),
        compiler_params=_params(("parallel", "parallel"), 40 * 1024 * 1024),
        name="s5_in",
    )(x, mods, e8, sc)

    states = pl.pallas_call(
        _s5n_scan_kernel,
        grid=(nlb, 2),
        in_specs=[
            pl.BlockSpec((1, 1, 2, bsz, rows, 2 * p), lambda gi, dr: (gi, dr, 0, 0, 0, 0)),
            pl.BlockSpec((1, 2, S5_LB, 2 * p), lambda gi, dr: (gi, dr, 0, 0)),
        ],
        out_specs=pl.BlockSpec((1, 1, bsz, rows, 2 * p), lambda gi, dr: (gi, dr, 0, 0, 0)),
        out_shape=st_shape,
        compiler_params=_params(("parallel", "parallel")),
        name="s5_scan",
    )(e, lam)

    return pl.pallas_call(
        _s5n_out_kernel,
        grid=(nlb, bsz),
        in_specs=[x_spec, m_spec, pl.BlockSpec((1, lw), lambda gi, b: (0, gi)),
                  per_block(k2), st_spec, per_block(w8), per_block(sc)],
        out_specs=x_spec,
        out_shape=jax.ShapeDtypeStruct((t, d), F32),
        compiler_params=_params(("parallel", "parallel"), 52 * 1024 * 1024),
        name="s5_out",
    )(x, mods, d_skip.reshape(1, d), k2, states, w8, sc)


def _row_spec(tm, d, col=0):
    return pl.BlockSpec((tm, d), lambda i: (i, col))


def _na_mixer(x, mods, w_qkv, rpb, w_out, ln_g, ln_b, bsz, seq):
    d = D_MODEL
    scale = jnp.concatenate([jnp.full((d,), NA_HEAD_DIM ** -0.5, F32), jnp.ones((2 * d,), F32)])
    qkv = _mod_matmul(x, mods, (w_qkv * scale).astype(BF16), BF16, seq)
    return _na_core(qkv, _na_bias_table(rpb), x, mods, w_out.astype(BF16), ln_g, ln_b, bsz, seq)


def _hgrn_mixer(x, mods, w_in, lbs, norm_g, w_out, ln_g, ln_b, bsz, seq):
    d = D_MODEL
    proj = _mod_matmul(x, mods, w_in.astype(BF16), F32, seq)
    o_f, o_b = _gla(proj, lbs, bsz, seq)
    tm = min(OUT_TM, seq)
    lead = [_row_spec(tm, d), _row_spec(tm, d), _row_spec(tm, d, 4)]
    return _out_call(_out_hgrn_kernel, "out_hgrn", lead, [o_f, o_b, proj], x, mods,
                     [pl.BlockSpec((1, d), lambda i: (0, 0))], [norm_g.reshape(1, d)],
                     w_out.astype(BF16), ln_g, ln_b, seq)


def _s5_mixer(x, mods, coefs, d_skip, w_out, ln_g, ln_b, bsz, seq):
    d = D_MODEL
    act = _s5n_core(x, mods, coefs, d_skip, bsz, seq)
    tm = min(OUT_TM, seq)
    return _out_call(_out_s5_kernel, "out_s5", [_row_spec(tm, d)], [act], x, mods, [], [],
                     w_out.astype(BF16), ln_g, ln_b, seq)


def _hgrn_lower_bounds(lb_logits):
    p = jax.nn.softmax(lb_logits.astype(F32), axis=1)
    cs = jnp.cumsum(p, axis=1)
    return cs - cs[:, :1]


def kernel(x, c, ada_w, ada_b, ln_g, ln_b, ffn_w_in, ffn_w_out, na_w_qkv, na_rpb, na_w_out,
           hgrn_w_in, hgrn_lb_logits, hgrn_norm_g, hgrn_w_out,
           s5_a_re, s5_a_im, s5_log_dt, s5_b_re, s5_b_im, s5_c_re, s5_c_im, s5_d, s5_w_out):
    bsz, seq, d = x.shape
    depth = ada_w.shape[0]
    mods_all = _ada_mods(c, ada_w, ada_b)
    lbs = _hgrn_lower_bounds(hgrn_lb_logits)
    xf = x.reshape(bsz * seq, d)
    w_in_bf = ffn_w_in.astype(BF16)
    w_out_bf = ffn_w_out.astype(BF16)
    for i in range(depth):
        kind = i % N_MIXERS
        j = i // N_MIXERS
        mods = mods_all[i]
        xf = _ffn(xf, mods, w_in_bf, w_out_bf, i, 0, ln_g[i, 0], ln_b[i, 0], 0, seq)
        if kind == 0:
            xf = _na_mixer(xf, mods, na_w_qkv[j], na_rpb[j], na_w_out[j], ln_g[i, 1], ln_b[i, 1], bsz, seq)
        elif kind == 1:
            xf = _hgrn_mixer(xf, mods, hgrn_w_in[j], lbs[:, i], hgrn_norm_g[j], hgrn_w_out[j],
                             ln_g[i, 1], ln_b[i, 1], bsz, seq)
        else:
            coefs = _s5n_coefs(s5_a_re[j], s5_a_im[j], s5_log_dt[j], s5_b_re[j], s5_b_im[j],
                               s5_c_re[j], s5_c_im[j])
            xf = _s5_mixer(xf, mods, coefs, s5_d[j], s5_w_out[j], ln_g[i, 1], ln_b[i, 1], bsz, seq)
        xf = _ffn(xf, mods, w_in_bf, w_out_bf, i, 1, ln_g[i, 2], ln_b[i, 2], 2, seq)
    return xf.reshape(bsz, seq, d)
```

```python
import functools
import math

import numpy as np
import jax
import jax.numpy as jnp
from jax import lax
from jax.experimental import pallas as pl
from jax.experimental.pallas import tpu as pltpu

F32 = jnp.float32
BF16 = jnp.bfloat16

D_MODEL = 1024
DEPTH = 4
N_MIXERS = 3
D_FF = 2816
LN_EPS = 1e-5
ALPHA = (2 * DEPTH) ** 0.25
GRID_W = 64
NA_HEADS = 16
NA_HEAD_DIM = D_MODEL // NA_HEADS
NA_ROWS = 8
NA_COLS = 16
HGRN_HEAD_DIM = 128
HGRN_HEADS = D_MODEL // HGRN_HEAD_DIM
HGRN_CHUNK = 64
S5_GROUP_CH = 16
S5_STATE = 64
S5_GROUPS = D_MODEL // S5_GROUP_CH

V7X_VMEM_BYTES = 64 * 1024 * 1024
VMEM_LIMIT = 56 * 1024 * 1024
FFN_TM = 1024
FFN_SPLIT = 4
MM_TM = 512
MM_TN = 1024
OUT_TM = 512
ADA_TN = 1152
NA_RB = 8
NA_UNROLL = 2
GLA_CB = 4
GLA_NB = 2
S5_L = 16
S5_R = 8
S5_LB = 8


def _params(sem, limit=VMEM_LIMIT):
    return pltpu.CompilerParams(dimension_semantics=sem, vmem_limit_bytes=limit)


def _layer_norm(z, g, b):
    mu = jnp.mean(z, axis=-1, keepdims=True)
    zc = z - mu
    var = jnp.mean(zc * zc, axis=-1, keepdims=True)
    return zc * lax.rsqrt(var + LN_EPS) * g + b


def _silu(a):
    return a * jax.nn.sigmoid(a)


def _ada_kernel(c_ref, w_ref, b_ref, o_ref):
    c = c_ref[...]
    sc = _silu(c).astype(BF16)
    w = w_ref[0].astype(BF16)
    o_ref[0] = jnp.dot(sc, w, preferred_element_type=F32) + b_ref[0]


def _ada_mods(c, ada_w, ada_b):
    bsz = c.shape[0]
    depth, d, n = ada_w.shape
    rows = 8
    cp = jnp.zeros((rows, d), F32).at[:bsz].set(c)
    out = pl.pallas_call(
        _ada_kernel,
        grid=(depth, n // ADA_TN),
        in_specs=[
            pl.BlockSpec((rows, d), lambda l, j: (0, 0)),
            pl.BlockSpec((1, d, ADA_TN), lambda l, j: (l, 0, j)),
            pl.BlockSpec((1, 1, ADA_TN), lambda l, j: (l, 0, j)),
        ],
        out_specs=pl.BlockSpec((1, rows, ADA_TN), lambda l, j: (l, 0, j)),
        out_shape=jax.ShapeDtypeStruct((depth, rows, n), F32),
        compiler_params=_params(("arbitrary", "arbitrary"), 32 * 1024 * 1024),
        name="ada",
    )(cp, ada_w, ada_b.reshape(depth, 1, n))
    return out[:, :bsz].reshape(depth, bsz, 9, d)


def _ffn_kernel(x_ref, m_ref, wa_ref, wb_ref, wo_ref, g_ref, b_ref, o_ref, *, sub):
    shift = m_ref[0, 3 * sub:3 * sub + 1, :]
    scale = m_ref[0, 3 * sub + 1:3 * sub + 2, :]
    gate = m_ref[0, 3 * sub + 2:3 * sub + 3, :]
    ts = x_ref.shape[0] // FFN_SPLIT
    rows = [slice(s * ts, (s + 1) * ts) for s in range(FFN_SPLIT)]
    def up(r):
        h = (x_ref[r, :] * (1.0 + scale) + shift).astype(BF16)
        return (jnp.dot(h, wa_ref[...], preferred_element_type=F32),
                jnp.dot(h, wb_ref[...], preferred_element_type=F32))

    nxt = up(rows[0])
    for s, r in enumerate(rows):
        a, b = nxt
        if s + 1 < len(rows):
            nxt = up(rows[s + 1])
        act = (_silu(a) * b).astype(BF16)
        y = jnp.dot(act, wo_ref[...], preferred_element_type=F32)
        z = ALPHA * x_ref[r, :] + (0.5 * gate) * y
        o_ref[r, :] = _layer_norm(z, g_ref[...], b_ref[...])


def _resident(shape, index_map):
    return pl.BlockSpec(shape, index_map, pipeline_mode=pl.Buffered(1))


def _ffn(x, mods, w_in, w_out, layer, half, ln_g, ln_b, sub, seq):
    t, d = x.shape
    dff = w_out.shape[2]
    tm = min(FFN_TM, seq)
    per_b = seq // tm
    return pl.pallas_call(
        functools.partial(_ffn_kernel, sub=sub),
        grid=(t // tm,),
        in_specs=[
            pl.BlockSpec((tm, d), lambda i: (i, 0)),
            pl.BlockSpec((1, 9, d), lambda i: (i // per_b, 0, 0)),
            _resident((None, None, d, dff), lambda i: (layer, half, 0, 0)),
            _resident((None, None, d, dff), lambda i: (layer, half, 0, 1)),
            _resident((None, None, dff, d), lambda i: (layer, half, 0, 0)),
            _resident((1, d), lambda i: (0, 0)),
            _resident((1, d), lambda i: (0, 0)),
        ],
        out_specs=pl.BlockSpec((tm, d), lambda i: (i, 0)),
        out_shape=jax.ShapeDtypeStruct((t, d), F32),
        compiler_params=_params(("parallel",)),
        name="ffn",
    )(x, mods, w_in, w_in, w_out, ln_g.reshape(1, d), ln_b.reshape(1, d))


def _mm_kernel(x_ref, m_ref, w_ref, o_ref):
    shift = m_ref[0, 3:4, :]
    scale = m_ref[0, 4:5, :]
    h = (x_ref[...] * (1.0 + scale) + shift).astype(BF16)
    for j in range(w_ref.shape[1] // MM_TN):
        cols = slice(j * MM_TN, (j + 1) * MM_TN)
        o_ref[:, cols] = jnp.dot(h, w_ref[:, cols], preferred_element_type=F32).astype(o_ref.dtype)


def _mod_matmul(x, mods, w, out_dtype, seq):
    t, d = x.shape
    n = w.shape[1]
    tm = min(MM_TM, seq)
    per_b = seq // tm
    return pl.pallas_call(
        _mm_kernel,
        grid=(t // tm,),
        in_specs=[
            pl.BlockSpec((tm, d), lambda i: (i, 0)),
            pl.BlockSpec((1, 9, d), lambda i: (i // per_b, 0, 0)),
            _resident((d, n), lambda i: (0, 0)),
        ],
        out_specs=pl.BlockSpec((tm, n), lambda i: (i, 0)),
        out_shape=jax.ShapeDtypeStruct((t, n), out_dtype),
        compiler_params=_params(("parallel",), 48 * 1024 * 1024),
        name="mm",
    )(x, mods, w)


def _mixer_epilogue(x_ref, m_ref, y, g_ref, b_ref, o_ref):
    gate = m_ref[0, 5:6, :]
    z = ALPHA * x_ref[...] + gate * y
    o_ref[...] = _layer_norm(z, g_ref[...], b_ref[...])


def _out_hgrn_kernel(of_ref, ob_ref, x_ref, m_ref, ng_ref, wg_ref, w_ref, g_ref, b_ref, o_ref):
    h = (x_ref[...] * (1.0 + m_ref[0, 4:5, :]) + m_ref[0, 3:4, :]).astype(BF16)
    gate = jnp.dot(h, wg_ref[...], preferred_element_type=F32)
    o = of_ref[...] + ob_ref[...]
    parts = []
    for h in range(HGRN_HEADS):
        oh = o[:, h * HGRN_HEAD_DIM:(h + 1) * HGRN_HEAD_DIM]
        ms = jnp.mean(oh * oh, axis=-1, keepdims=True)
        parts.append(oh * lax.rsqrt(ms + LN_EPS))
    on = jnp.concatenate(parts, axis=1)
    lhs = (on * ng_ref[...] * _silu(gate)).astype(BF16)
    y = jnp.dot(lhs, w_ref[...], preferred_element_type=F32)
    _mixer_epilogue(x_ref, m_ref, y, g_ref, b_ref, o_ref)


def _out_s5_kernel(a_ref, x_ref, m_ref, w_ref, g_ref, b_ref, o_ref):
    d = x_ref.shape[1]
    yy = jnp.dot(a_ref[...].astype(BF16), w_ref[...], preferred_element_type=F32)
    y = yy[:, :d] * jax.nn.sigmoid(yy[:, d:])
    _mixer_epilogue(x_ref, m_ref, y, g_ref, b_ref, o_ref)


def _out_call(kernel, name, lead_specs, lead_args, x, mods, mid_specs, mid_args, w, ln_g, ln_b, seq):
    t, d = x.shape
    tm = min(OUT_TM, seq)
    per_b = seq // tm
    in_specs = list(lead_specs) + [
        pl.BlockSpec((tm, d), lambda i: (i, 0)),
        pl.BlockSpec((1, 9, d), lambda i: (i // per_b, 0, 0)),
    ] + list(mid_specs) + [
        _resident(w.shape, lambda i: (0, 0)),
        _resident((1, d), lambda i: (0, 0)),
        _resident((1, d), lambda i: (0, 0)),
    ]
    return pl.pallas_call(
        kernel,
        grid=(t // tm,),
        in_specs=in_specs,
        out_specs=pl.BlockSpec((tm, d), lambda i: (i, 0)),
        out_shape=jax.ShapeDtypeStruct((t, d), F32),
        compiler_params=_params(("parallel",), 40 * 1024 * 1024),
        name=name,
    )(*lead_args, x, mods, *mid_args, w, ln_g.reshape(1, d), ln_b.reshape(1, d))


def _na_bias_table(rpb):
    qc = np.arange(GRID_W)[:, None]
    kc = np.arange(GRID_W)[None, :]
    cs = np.clip(qc - NA_COLS // 2, 0, GRID_W - NA_COLS)
    mask = (kc >= cs) & (kc < cs + NA_COLS)
    dc = np.clip(kc - qc + NA_COLS - 1, 0, 2 * NA_COLS - 2)
    onehot = (dc.reshape(1, -1) == np.arange(2 * NA_COLS - 1)[:, None]).astype(np.float32)
    cols = jnp.dot(rpb.astype(F32).reshape(-1, 2 * NA_COLS - 1), jnp.asarray(onehot),
                   precision=lax.Precision.HIGHEST)
    cols = cols.reshape(NA_HEADS, 2 * NA_ROWS - 1, GRID_W, GRID_W)
    tbl = jnp.where(jnp.asarray(mask)[None, None], cols, -jnp.inf)
    tbl2 = jnp.concatenate([tbl[:, :-1], tbl[:, 1:]], axis=-1)
    return tbl2.reshape(NA_HEADS * (2 * NA_ROWS - 2), GRID_W, 2 * GRID_W)


def _na_kernel(q_ref, kp_ref, kc_ref, kn_ref, vp_ref, vc_ref, vn_ref, tbl_ref,
               x_ref, m_ref, wo_ref, g_ref, b_ref, o_ref, k_scr, v_scr, att_scr, *, rows):
    w = GRID_W
    half = NA_RB * w // 2
    full = NA_RB * w
    k_scr[0:half] = kp_ref[...]
    k_scr[half:half + full] = kc_ref[...]
    k_scr[half + full:] = kn_ref[...]
    v_scr[0:half] = vp_ref[...]
    v_scr[half:half + full] = vc_ref[...]
    v_scr[half + full:] = vn_ref[...]

    r0 = pl.program_id(1) * NA_RB
    lane = lax.broadcasted_iota(jnp.int32, (w, 2 * NA_HEAD_DIM), 1)
    lo = lane < NA_HEAD_DIM
    nd = 2 * NA_ROWS - 2

    def pair_slice(h):
        return slice(2 * NA_HEAD_DIM * (h // 2), 2 * NA_HEAD_DIM * (h // 2 + 1))

    def geometry(rr):
        r = r0 + rr
        rs = jnp.clip(r - NA_ROWS // 2, 0, rows - NA_ROWS)
        woff = pl.multiple_of((rs - r0 + NA_ROWS // 2) * w, w)
        return woff, r - rs, pl.multiple_of(rr * w, w)

    def score_pass(geo):
        woff, _, qoff = geo
        scores = []
        for hp in range(NA_HEADS // 2):
            sl = pair_slice(2 * hp)
            q2 = q_ref[pl.ds(qoff, w), sl]
            k2 = k_scr[pl.ds(woff, NA_ROWS * w), sl]
            zero = jnp.zeros_like(q2)
            qm = jnp.concatenate([jnp.where(lo, q2, zero), jnp.where(lo, zero, q2)], axis=0)
            scores.append(lax.dot_general(qm, k2, (((1,), (1,)), ((), ())), preferred_element_type=F32))
        return scores

    def softmax_value_pass(geo, scores):
        woff, rel, qoff = geo
        probs, denom = [], []
        for hp in range(NA_HEADS // 2):
            bias = jnp.concatenate(
                [jnp.concatenate([tbl_ref[h * nd + 2 * j + (NA_ROWS - 1) - rel] for j in range(NA_ROWS // 2)],
                                 axis=1) for h in (2 * hp, 2 * hp + 1)], axis=0)
            s = scores[hp] + bias
            m = jnp.max(s, axis=1, keepdims=True)
            p = jnp.exp(s - m)
            denom.append(jnp.sum(p, axis=1, keepdims=True))
            probs.append(p.astype(BF16))
        outs = []
        for hp in range(NA_HEADS // 2):
            v2 = v_scr[pl.ds(woff, NA_ROWS * w), pair_slice(2 * hp)]
            o = jnp.dot(probs[hp], v2, preferred_element_type=F32) / denom[hp]
            outs.append(jnp.where(lo, o[:w], o[w:]))
        att_scr[pl.ds(qoff, w), :] = jnp.concatenate(outs, axis=1).astype(att_scr.dtype)

    def rows_body(it, carry):
        geos = [geometry(it * NA_UNROLL + u) for u in range(NA_UNROLL)]
        scores = [score_pass(g) for g in geos]
        for g, s in zip(geos, scores):
            softmax_value_pass(g, s)
        return carry

    lax.fori_loop(0, NA_RB // NA_UNROLL, rows_body, 0)
    y = jnp.dot(att_scr[...], wo_ref[...], preferred_element_type=F32)
    _mixer_epilogue(x_ref, m_ref, y, g_ref, b_ref, o_ref)


def _na_core(qkv, tbl, x, mods, w_out, ln_g, ln_b, bsz, seq):
    t = qkv.shape[0]
    d = D_MODEL
    rows = seq // GRID_W
    nrb = rows // NA_RB
    full = NA_RB * GRID_W
    half = full // 2

    def cur(col):
        return pl.BlockSpec((full, d), lambda b, i: (b * nrb + i, col))

    def prev(col):
        return pl.BlockSpec((half, d), lambda b, i: (b * 2 * nrb + jnp.maximum(2 * i - 1, 0), col))

    def nxt(col):
        return pl.BlockSpec((half, d), lambda b, i: (b * 2 * nrb + jnp.minimum(2 * i + 2, 2 * nrb - 1), col))

    return pl.pallas_call(
        functools.partial(_na_kernel, rows=rows),
        grid=(bsz, nrb),
        in_specs=[cur(0), prev(1), cur(1), nxt(1), prev(2), cur(2), nxt(2),
                  _resident(tbl.shape, lambda b, i: (0, 0, 0)),
                  cur(0),
                  pl.BlockSpec((1, 9, d), lambda b, i: (b, 0, 0)),
                  _resident((d, d), lambda b, i: (0, 0)),
                  _resident((1, d), lambda b, i: (0, 0)),
                  _resident((1, d), lambda b, i: (0, 0))],
        out_specs=pl.BlockSpec((full, d), lambda b, i: (b * nrb + i, 0)),
        out_shape=jax.ShapeDtypeStruct((t, d), F32),
        scratch_shapes=[pltpu.VMEM((2 * full, d), BF16), pltpu.VMEM((2 * full, d), BF16),
                        pltpu.VMEM((full, d), BF16)],
        compiler_params=_params(("parallel", "arbitrary")),
        name="na",
    )(qkv, qkv, qkv, qkv, qkv, qkv, qkv, tbl, x, mods, w_out, ln_g.reshape(1, d), ln_b.reshape(1, d))


def _gla_kernel(qf_ref, vf_ref, ff_ref, qb_ref, vb_ref, fb_ref, lb_ref, of_ref, ob_ref, st_scr):
    c = HGRN_CHUNK
    dh = HGRN_HEAD_DIM
    ns = 2 * GLA_NB
    tdims = (((1,), (1,)), ((), ()))
    in_refs = ((qf_ref, vf_ref, ff_ref), (qb_ref, vb_ref, fb_ref))
    out_refs = (of_ref, ob_ref)

    @pl.when(pl.program_id(1) == 0)
    def _():
        st_scr[...] = jnp.zeros_like(st_scr)

    row = lax.broadcasted_iota(jnp.int32, (c, c), 0)
    col = lax.broadcasted_iota(jnp.int32, (c, c), 1)
    tris = (col <= row, col >= row)
    tri3 = [jnp.concatenate([t.astype(F32).astype(BF16)] * 3, axis=1) for t in tris]
    mid_row = (c // 2, c // 2 - 1)
    last_row = (c - 1, 0)

    def chunk_body(ci, carry):
        offs = (pl.multiple_of(ci * c, c), pl.multiple_of((GLA_CB - 1 - ci) * c, c))
        pre = []
        for s in range(ns):
            bb, dr = s // 2, s % 2
            q_ref, v_ref, f_ref = in_refs[dr]
            lbv = lb_ref[dr]
            f = lbv + (1.0 - lbv) * jax.nn.sigmoid(f_ref[bb, pl.ds(offs[dr], c), :])
            g = jnp.log(f)
            k = 1.0 - f
            g1 = g.astype(BF16)
            r1 = g - g1.astype(F32)
            g2 = r1.astype(BF16)
            g3 = (r1 - g2.astype(F32)).astype(BF16)
            bc = jnp.dot(tri3[dr], jnp.concatenate([g1, g2, g3], axis=0), preferred_element_type=F32)
            b_mid = bc[mid_row[dr]:mid_row[dr] + 1]
            b_last = bc[last_row[dr]:last_row[dr] + 1]
            hq = 0.5 * q_ref[bb, pl.ds(offs[dr], c), :]
            qs = hq + hq * jnp.tanh(hq)
            pre.append(dict(
                qe=(qs * jnp.exp(bc - b_mid)).astype(BF16),
                ke=(k * jnp.exp(b_mid - bc)).astype(BF16),
                qd=(qs * jnp.exp(bc)).astype(BF16),
                kd=(k * jnp.exp(b_last - bc)).astype(BF16),
                dec=jnp.exp(b_last),
                v=v_ref[bb, pl.ds(offs[dr], c), :].astype(BF16)))
        units = [(s, h) for s in range(ns) for h in range(HGRN_HEADS)]
        sls = [slice(h * dh, (h + 1) * dh) for h in range(HGRN_HEADS)]
        att = [lax.dot_general(pre[s]["qe"][:, sls[h]], pre[s]["ke"][:, sls[h]], tdims,
                               preferred_element_type=F32) for s, h in units]
        inter, upd = [], []
        for s, h in units:
            st = st_scr[s, sls[h], :]
            inter.append(lax.dot_general(pre[s]["qd"][:, sls[h]], st.astype(BF16), tdims,
                                         preferred_element_type=F32))
            new = lax.dot_general(pre[s]["v"][:, sls[h]], pre[s]["kd"][:, sls[h]],
                                  (((0,), (0,)), ((), ())), preferred_element_type=F32)
            upd.append(pre[s]["dec"][:, sls[h]] * st + new)
        for u, (s, h) in enumerate(units):
            st_scr[s, sls[h], :] = upd[u]
        for u, (s, h) in enumerate(units):
            a = jnp.where(tris[s % 2], att[u], 0.0).astype(BF16)
            o = jnp.dot(a, pre[s]["v"][:, sls[h]], preferred_element_type=F32) + inter[u]
            out_refs[s % 2][s // 2, pl.ds(offs[s % 2], c), sls[h]] = o
        return carry

    lax.fori_loop(0, GLA_CB, chunk_body, 0)


def _gla(proj, lbs, bsz, seq):
    t = proj.shape[0]
    d = D_MODEL
    tb = GLA_CB * HGRN_CHUNK
    nblk = seq // tb
    nb = min(GLA_NB, bsz)
    assert nb == GLA_NB and bsz % nb == 0
    proj3 = proj.reshape(bsz, seq, proj.shape[1])

    def fwd(col):
        return pl.BlockSpec((nb, tb, d), lambda b, i: (b, i, col))

    def bwd(col):
        return pl.BlockSpec((nb, tb, d), lambda b, i: (b, nblk - 1 - i, col))

    o_f, o_b = pl.pallas_call(
        _gla_kernel,
        grid=(bsz // nb, nblk),
        in_specs=[fwd(0), fwd(1), fwd(2), bwd(0), bwd(1), bwd(3),
                  pl.BlockSpec((2, 1, d), lambda b, i: (0, 0, 0))],
        out_specs=[fwd(0), bwd(0)],
        out_shape=[jax.ShapeDtypeStruct((bsz, seq, d), F32)] * 2,
        scratch_shapes=[pltpu.VMEM((2 * nb, d, HGRN_HEAD_DIM), F32)],
        compiler_params=_params(("parallel", "arbitrary"), 48 * 1024 * 1024),
        name="gla",
    )(proj3, proj3, proj3, proj3, proj3, proj3, lbs.reshape(2, 1, d))
    return o_f.reshape(t, d), o_b.reshape(t, d)


def _gelu_tanh(y):
    return 0.5 * y * (1.0 + jnp.tanh(math.sqrt(2.0 / math.pi) * (y + 0.044715 * (y * y * y))))


def _s5n_coef_kernel(a_ref, bt_ref, c_ref, k2_ref, e8_ref, w8_ref, sc_ref, lam_ref):
    L, ch, p, R = S5_L, S5_GROUP_CH, S5_STATE, S5_R
    lanes = 2 * p
    hi = lax.Precision.HIGHEST
    e8_ref[...] = jnp.zeros_like(e8_ref)
    w8_ref[...] = jnp.zeros_like(w8_ref)
    for g8 in range(S5_LB):
        a = a_ref[g8]
        cr, ci = c_ref[g8, 0], c_ref[g8, 1]
        br, bi = bt_ref[g8, 0], bt_ref[g8, 1]
        zb, ang_dt = [], []
        for dr in range(2):
            ar = a[3 * dr:3 * dr + 1]
            ai = a[3 * dr + 1:3 * dr + 2]
            dt = jnp.exp(a[3 * dr + 2:3 * dr + 3])
            mag = jnp.exp(ar * dt)
            ang = ai * dt
            lr = mag * jnp.cos(ang)
            li = mag * jnp.sin(ang)
            den = ar * ar + ai * ai
            zr = ((lr - 1.0) * ar + li * ai) / den
            zi = (li * ar - (lr - 1.0) * ai) / den
            zb.append((zr * br - zi * bi, zr * bi + zi * br))
            ang_dt.append((ar * dt, ang))

        def powers(dr, m):
            lre, lang = ang_dt[dr]
            mg = jnp.exp(m * lre)
            return mg * jnp.cos(m * lang), mg * jnp.sin(m * lang)

        def outer(pw, coef, sign=1.0):
            pr, pi = pw
            fr, fi = coef
            r = pr.shape[0]
            re = pr[:, None, :] * fr[None] - pi[:, None, :] * fi[None]
            im = pr[:, None, :] * fi[None] + pi[:, None, :] * fr[None]
            return jnp.concatenate([re.reshape(r * ch, p), sign * im.reshape(r * ch, p)], axis=1)

        rows = slice(g8 * ch, (g8 + 1) * ch)
        n = lax.broadcasted_iota(jnp.int32, (2 * L, 1), 0)
        ccp = [jnp.zeros((g8 * ch, lanes), F32), jnp.concatenate([cr, -ci], axis=1),
               jnp.zeros(((S5_LB - 1 - g8) * ch, lanes), F32)]
        ccp = jnp.concatenate([v for v in ccp if v.shape[0]], axis=0)
        kn = jnp.zeros((2 * L * ch, S5_LB * ch), F32)
        for dr in range(2):
            lag = ((L - 1) - n) if dr == 0 else (n - (L - 1))
            valid = jnp.logical_and(lag >= 0, n <= 2 * L - 2).astype(F32)
            pr, pi = powers(dr, jnp.maximum(lag, 0).astype(F32))
            zbp = outer((pr * valid, pi * valid), zb[dr])
            kn = kn + lax.dot_general(zbp, ccp, (((1,), (1,)), ((), ())), precision=hi,
                                      preferred_element_type=F32)
        kn3 = kn.reshape(2 * L, ch, S5_LB * ch).astype(k2_ref.dtype)
        half = S5_LB * ch
        k2_ref[0, :, rows, 0:half] = kn3
        k2_ref[0, 1:, rows, half:] = kn3[:-1]
        k2_ref[0, 0:1, rows, half:] = jnp.zeros((1, ch, half), k2_ref.dtype)

        rr = lax.broadcasted_iota(jnp.int32, (R, 1), 0).astype(F32)
        pieces = (
            (e8_ref, outer(powers(0, (R - 1) - rr), zb[0]), outer(powers(1, rr), zb[1])),
            (w8_ref, outer(powers(0, rr + 1.0), (cr, ci), -1.0), outer(powers(1, R - rr), (cr, ci), -1.0)),
        )
        for ref, fwd, bwd in pieces:
            for dr, piece in enumerate((fwd, bwd)):
                col = (2 * g8 + dr) * lanes
                ref[0, :, rows, col:col + lanes] = piece.reshape(R, ch, lanes).astype(ref.dtype)

        lane = lax.broadcasted_iota(jnp.int32, (1, lanes), 1)
        sign = jnp.where(lane < p, -1.0, 1.0)

        def mul_rows(dr, m):
            pr, pi = powers(dr, jnp.full((1, 1), float(m), F32))
            return jnp.concatenate([pr, pr], axis=1), jnp.concatenate([pi, pi], axis=1) * sign

        na = L // R
        for a_i in range(na):
            for dr, m in ((0, R * a_i), (1, R * (na - 1 - a_i))):
                col = (2 * g8 + dr) * lanes
                ca, cb = mul_rows(dr, m)
                sc_ref[0, a_i:a_i + 1, col:col + lanes] = ca
                sc_ref[0, na + a_i:na + a_i + 1, col:col + lanes] = cb
        for dr in range(2):
            ca, cb = mul_rows(dr, L)
            lam_ref[0, 2 * dr, g8:g8 + 1, :] = ca
            lam_ref[0, 2 * dr + 1, g8:g8 + 1, :] = cb


def _s5n_coefs(a_re, a_im, log_dt, b_re, b_im, c_re, c_im):
    g, p, ch, L, R = S5_GROUPS, S5_STATE, S5_GROUP_CH, S5_L, S5_R
    nlb = g // S5_LB
    lw = S5_LB * ch
    sw = S5_LB * 4 * p
    ldt = jnp.broadcast_to(log_dt[:, :, None], (2, g, p))
    zero = jnp.zeros((g, p), F32)
    amat = jnp.stack([a_re[0], a_im[0], ldt[0], a_re[1], a_im[1], ldt[1], zero, zero], axis=1)
    bt = jnp.stack([b_re, b_im], axis=1).transpose(0, 1, 3, 2)
    cc = jnp.stack([c_re, c_im], axis=1)
    k2, e8, w8, sc, lam = pl.pallas_call(
        _s5n_coef_kernel,
        grid=(nlb,),
        in_specs=[
            pl.BlockSpec((S5_LB, 8, p), lambda i: (i, 0, 0)),
            pl.BlockSpec((S5_LB, 2, ch, p), lambda i: (i, 0, 0, 0)),
            pl.BlockSpec((S5_LB, 2, ch, p), lambda i: (i, 0, 0, 0)),
        ],
        out_specs=[
            pl.BlockSpec((1, 2 * L, lw, 2 * lw), lambda i: (i, 0, 0, 0)),
            pl.BlockSpec((1, R, lw, sw), lambda i: (i, 0, 0, 0)),
            pl.BlockSpec((1, R, lw, sw), lambda i: (i, 0, 0, 0)),
            pl.BlockSpec((1, 2 * (L // R), sw), lambda i: (i, 0, 0)),
            pl.BlockSpec((1, 4, S5_LB, 2 * p), lambda i: (i, 0, 0, 0)),
        ],
        out_shape=[
            jax.ShapeDtypeStruct((nlb, 2 * L, lw, 2 * lw), BF16),
            jax.ShapeDtypeStruct((nlb, R, lw, sw), BF16),
            jax.ShapeDtypeStruct((nlb, R, lw, sw), BF16),
            jax.ShapeDtypeStruct((nlb, 2 * (L // R), sw), F32),
            jax.ShapeDtypeStruct((nlb, 4, S5_LB, 2 * p), F32),
        ],
        compiler_params=_params(("parallel",), 48 * 1024 * 1024),
        name="s5_coef",
    )(amat, bt, cc)
    return k2.reshape(nlb, 2 * L * lw, 2 * lw), e8.reshape(nlb, R * lw, sw), w8.reshape(nlb, R * lw, sw), sc, lam


def _swap_halves(v):
    w = 2 * S5_STATE
    return jnp.concatenate([pltpu.roll(v[:, k * w:(k + 1) * w], S5_STATE, 1) for k in range(v.shape[1] // w)],
                           axis=1)


def _s5n_load_chunks(x_ref, m_ref, s):
    tm = x_ref.shape[0] // S5_L
    return x_ref[pl.ds(s, tm, stride=S5_L), :] * (1.0 + m_ref[0, 4:5, :]) + m_ref[0, 3:4, :]


def _s5n_in_kernel(x_ref, m_ref, e8_ref, sc_ref, e_ref):
    L, R, w = S5_L, S5_R, 2 * S5_STATE
    na = L // R
    tm = x_ref.shape[0] // L
    acc = jnp.zeros((tm, e8_ref.shape[2]), F32)
    for a in range(na):
        xa = jnp.concatenate([_s5n_load_chunks(x_ref, m_ref, R * a + r).astype(BF16) for r in range(R)], axis=1)
        ea = jnp.dot(xa, e8_ref[0], preferred_element_type=F32)
        ta = na - 1 - a
        acc = acc + sc_ref[0, ta:ta + 1, :] * ea + sc_ref[0, na + ta:na + ta + 1, :] * _swap_halves(ea)
    for plane, val in enumerate((acc, _swap_halves(acc))):
        for g8 in range(S5_LB):
            for dr in range(2):
                col = (2 * g8 + dr) * w
                e_ref[0, dr, plane, 0, pl.ds(g8, tm, stride=S5_LB), :] = val[:, col:col + w]


def _s5n_scan_kernel(e_ref, lam_ref, s_ref):
    nb = e_ref.shape[3]
    sub = S5_LB
    nch = e_ref.shape[4] // sub
    dr = pl.program_id(1)
    ca, cb = lam_ref[0, 0], lam_ref[0, 1]

    def body(j, carry):
        off = pl.multiple_of(jnp.where(dr == 0, j, nch - 1 - j) * sub, sub)
        new = []
        for b in range(nb):
            x, xs = carry[2 * b], carry[2 * b + 1]
            s_ref[0, 0, b, pl.ds(off, sub), :] = x
            new.append(ca * x + cb * xs + e_ref[0, 0, 0, b, pl.ds(off, sub), :])
            new.append(ca * xs - cb * x + e_ref[0, 0, 1, b, pl.ds(off, sub), :])
        return tuple(new)

    zero = jnp.zeros((sub, 2 * S5_STATE), F32)
    lax.fori_loop(0, nch, body, tuple([zero] * (2 * nb)))


def _s5n_out_kernel(x_ref, m_ref, d_ref, k2_ref, s_ref, w8_ref, sc_ref, y_ref):
    L, R, w = S5_L, S5_R, 2 * S5_STATE
    na = L // R
    lw = x_ref.shape[1]
    tm = x_ref.shape[0] // L
    xs = [_s5n_load_chunks(x_ref, m_ref, s) for s in range(L)]
    xflat = jnp.concatenate([v.astype(BF16) for v in xs], axis=1)
    st = jnp.concatenate([s_ref[0, dr, 0, pl.ds(g8, tm, stride=S5_LB), :]
                          for g8 in range(S5_LB) for dr in range(2)], axis=1)
    stsw = _swap_halves(st)
    corr = []
    for a in range(na):
        sa = (sc_ref[0, a:a + 1, :] * st + sc_ref[0, na + a:na + a + 1, :] * stsw).astype(BF16)
        corr.append(lax.dot_general(sa, w8_ref[0], (((1,), (1,)), ((), ())), preferred_element_type=F32))
    for tp in range(L // 2):
        start = (L - 1 - 2 * tp) * lw
        yp = jnp.dot(xflat, k2_ref[0, start:start + L * lw, :], preferred_element_type=F32)
        for q in range(2):
            t = 2 * tp + q
            y = yp[:, q * lw:(q + 1) * lw] + corr[t // R][:, (t % R) * lw:(t % R + 1) * lw] + d_ref[...] * xs[t]
            y_ref[pl.ds(t, tm, stride=L), :] = _gelu_tanh(y)


def _s5n_core(x, mods, coefs, d_skip, bsz, seq):
    k2, e8, w8, sc, lam = coefs
    t, d = x.shape
    L, p = S5_L, S5_STATE
    lw = S5_LB * S5_GROUP_CH
    nlb = d // lw
    mb = seq // L
    rows = mb * S5_LB

    x_spec = pl.BlockSpec((seq, lw), lambda gi, b: (b, gi))
    m_spec = pl.BlockSpec((1, 9, lw), lambda gi, b: (b, 0, gi))
    st_spec = pl.BlockSpec((1, 2, 1, rows, 2 * p), lambda gi, b: (gi, 0, b, 0, 0))
    st_shape = jax.ShapeDtypeStruct((nlb, 2, bsz, rows, 2 * p), F32)

    def per_block(arr):
        return pl.BlockSpec((1,) + arr.shape[1:], lambda gi, b: (gi,) + (0,) * (arr.ndim - 1),
                            pipeline_mode=pl.Buffered(1))

    e = pl.pallas_call(
        _s5n_in_kernel,
        grid=(nlb, bsz),
        in_specs=[x_spec, m_spec, per_block(e8), per_block(sc)],
        out_specs=pl.BlockSpec((1, 2, 2, 1, rows, 2 * p), lambda gi, b: (gi, 0, 0, b, 0, 0)),
        out_shape=jax.ShapeDtypeStruct((nlb, 2, 2, bsz, rows, 2 * p), F32),
        compiler_params=_params(("parallel", "parallel"), 40 * 1024 * 1024),
        name="s5_in",
    )(x, mods, e8, sc)

    states = pl.pallas_call(
        _s5n_scan_kernel,
        grid=(nlb, 2),
        in_specs=[
            pl.BlockSpec((1, 1, 2, bsz, rows, 2 * p), lambda gi, dr: (gi, dr, 0, 0, 0, 0)),
            pl.BlockSpec((1, 2, S5_LB, 2 * p), lambda gi, dr: (gi, dr, 0, 0)),
        ],
        out_specs=pl.BlockSpec((1, 1, bsz, rows, 2 * p), lambda gi, dr: (gi, dr, 0, 0, 0)),
        out_shape=st_shape,
        compiler_params=_params(("parallel", "parallel")),
        name="s5_scan",
    )(e, lam)

    return pl.pallas_call(
        _s5n_out_kernel,
        grid=(nlb, bsz),
        in_specs=[x_spec, m_spec, pl.BlockSpec((1, lw), lambda gi, b: (0, gi)),
                  per_block(k2), st_spec, per_block(w8), per_block(sc)],
        out_specs=x_spec,
        out_shape=jax.ShapeDtypeStruct((t, d), F32),
        compiler_params=_params(("parallel", "parallel"), 52 * 1024 * 1024),
        name="s5_out",
    )(x, mods, d_skip.reshape(1, d), k2, states, w8, sc)


def _row_spec(tm, d, col=0):
    return pl.BlockSpec((tm, d), lambda i: (i, col))


def _na_mixer(x, mods, w_qkv, rpb, w_out, ln_g, ln_b, bsz, seq):
    d = D_MODEL
    scale = jnp.concatenate([jnp.full((d,), NA_HEAD_DIM ** -0.5, F32), jnp.ones((2 * d,), F32)])
    qkv = _mod_matmul(x, mods, (w_qkv * scale).astype(BF16), BF16, seq)
    return _na_core(qkv, _na_bias_table(rpb), x, mods, w_out.astype(BF16), ln_g, ln_b, bsz, seq)


def _hgrn_mixer(x, mods, w_in, lbs, norm_g, w_out, ln_g, ln_b, bsz, seq):
    d = D_MODEL
    w_bf = w_in.astype(BF16)
    proj = _mod_matmul(x, mods, w_bf[:, :4 * d], F32, seq)
    o_f, o_b = _gla(proj, lbs, bsz, seq)
    tm = min(OUT_TM, seq)
    return _out_call(_out_hgrn_kernel, "out_hgrn", [_row_spec(tm, d), _row_spec(tm, d)], [o_f, o_b], x, mods,
                     [_resident((1, d), lambda i: (0, 0)), _resident((d, d), lambda i: (0, 0))],
                     [norm_g.reshape(1, d), w_bf[:, 4 * d:]],
                     w_out.astype(BF16), ln_g, ln_b, seq)


def _s5_mixer(x, mods, coefs, d_skip, w_out, ln_g, ln_b, bsz, seq):
    d = D_MODEL
    act = _s5n_core(x, mods, coefs, d_skip, bsz, seq)
    tm = min(OUT_TM, seq)
    return _out_call(_out_s5_kernel, "out_s5", [_row_spec(tm, d)], [act], x, mods, [], [],
                     w_out.astype(BF16), ln_g, ln_b, seq)


def _hgrn_lower_bounds(lb_logits):
    p = jax.nn.softmax(lb_logits.astype(F32), axis=1)
    cs = jnp.cumsum(p, axis=1)
    return cs - cs[:, :1]


def kernel(x, c, ada_w, ada_b, ln_g, ln_b, ffn_w_in, ffn_w_out, na_w_qkv, na_rpb, na_w_out,
           hgrn_w_in, hgrn_lb_logits, hgrn_norm_g, hgrn_w_out,
           s5_a_re, s5_a_im, s5_log_dt, s5_b_re, s5_b_im, s5_c_re, s5_c_im, s5_d, s5_w_out):
    bsz, seq, d = x.shape
    depth = ada_w.shape[0]
    mods_all = _ada_mods(c, ada_w, ada_b)
    lbs = _hgrn_lower_bounds(hgrn_lb_logits)
    xf = x.reshape(bsz * seq, d)
    w_in_bf = ffn_w_in.astype(BF16)
    w_out_bf = ffn_w_out.astype(BF16)
    for i in range(depth):
        kind = i % N_MIXERS
        j = i // N_MIXERS
        mods = mods_all[i]
        xf = _ffn(xf, mods, w_in_bf, w_out_bf, i, 0, ln_g[i, 0], ln_b[i, 0], 0, seq)
        if kind == 0:
            xf = _na_mixer(xf, mods, na_w_qkv[j], na_rpb[j], na_w_out[j], ln_g[i, 1], ln_b[i, 1], bsz, seq)
        elif kind == 1:
            xf = _hgrn_mixer(xf, mods, hgrn_w_in[j], lbs[:, i], hgrn_norm_g[j], hgrn_w_out[j],
                             ln_g[i, 1], ln_b[i, 1], bsz, seq)
        else:
            coefs = _s5n_coefs(s5_a_re[j], s5_a_im[j], s5_log_dt[j], s5_b_re[j], s5_b_im[j],
                               s5_c_re[j], s5_c_im[j])
            xf = _s5_mixer(xf, mods, coefs, s5_d[j], s5_w_out[j], ln_g[i, 1], ln_b[i, 1], bsz, seq)
        xf = _ffn(xf, mods, w_in_bf, w_out_bf, i, 1, ln_g[i, 2], ln_b[i, 2], 2, seq)
    return xf.reshape(bsz, seq, d)
```

```python
import functools
import math

import numpy as np
import jax
import jax.numpy as jnp
from jax import lax
from jax.experimental import pallas as pl
from jax.experimental.pallas import tpu as pltpu

F32 = jnp.float32
BF16 = jnp.bfloat16

D_MODEL = 1024
DEPTH = 4
N_MIXERS = 3
D_FF = 2816
LN_EPS = 1e-5
ALPHA = (2 * DEPTH) ** 0.25
GRID_W = 64
NA_HEADS = 16
NA_HEAD_DIM = D_MODEL // NA_HEADS
NA_ROWS = 8
NA_COLS = 16
HGRN_HEAD_DIM = 128
HGRN_HEADS = D_MODEL // HGRN_HEAD_DIM
HGRN_CHUNK = 64
S5_GROUP_CH = 16
S5_STATE = 64
S5_GROUPS = D_MODEL // S5_GROUP_CH

V7X_VMEM_BYTES = 64 * 1024 * 1024
VMEM_LIMIT = 56 * 1024 * 1024
FFN_TM = 1024
FFN_SPLIT = 4
MM_TM = 512
MM_TN = 1024
OUT_TM = 512
ADA_TN = 1152
NA_RB = 8
NA_UNROLL = 2
GLA_CB = 4
GLA_NB = 2
S5_L = 16
S5_R = 8
S5_LB = 8


def _params(sem, limit=VMEM_LIMIT):
    return pltpu.CompilerParams(dimension_semantics=sem, vmem_limit_bytes=limit)


def _layer_norm(z, g, b):
    mu = jnp.mean(z, axis=-1, keepdims=True)
    zc = z - mu
    var = jnp.mean(zc * zc, axis=-1, keepdims=True)
    return zc * lax.rsqrt(var + LN_EPS) * g + b


def _silu(a):
    return a * jax.nn.sigmoid(a)


def _ada_kernel(c_ref, w_ref, b_ref, o_ref):
    c = c_ref[...]
    sc = _silu(c).astype(BF16)
    w = w_ref[0].astype(BF16)
    o_ref[0] = jnp.dot(sc, w, preferred_element_type=F32) + b_ref[0]


def _ada_mods(c, ada_w, ada_b):
    bsz = c.shape[0]
    depth, d, n = ada_w.shape
    rows = 8
    cp = jnp.zeros((rows, d), F32).at[:bsz].set(c)
    out = pl.pallas_call(
        _ada_kernel,
        grid=(depth, n // ADA_TN),
        in_specs=[
            pl.BlockSpec((rows, d), lambda l, j: (0, 0)),
            pl.BlockSpec((1, d, ADA_TN), lambda l, j: (l, 0, j)),
            pl.BlockSpec((1, 1, ADA_TN), lambda l, j: (l, 0, j)),
        ],
        out_specs=pl.BlockSpec((1, rows, ADA_TN), lambda l, j: (l, 0, j)),
        out_shape=jax.ShapeDtypeStruct((depth, rows, n), F32),
        compiler_params=_params(("arbitrary", "arbitrary"), 32 * 1024 * 1024),
        name="ada",
    )(cp, ada_w, ada_b.reshape(depth, 1, n))
    return out[:, :bsz].reshape(depth, bsz, 9, d)


def _ffn_kernel(x_ref, m_ref, wa_ref, wb_ref, wo_ref, g_ref, b_ref, o_ref, *, sub):
    shift = m_ref[0, 3 * sub:3 * sub + 1, :]
    scale = m_ref[0, 3 * sub + 1:3 * sub + 2, :]
    gate = m_ref[0, 3 * sub + 2:3 * sub + 3, :]
    ts = x_ref.shape[0] // FFN_SPLIT
    rows = [slice(s * ts, (s + 1) * ts) for s in range(FFN_SPLIT)]
    def up(r):
        h = (x_ref[r, :] * (1.0 + scale) + shift).astype(BF16)
        return (jnp.dot(h, wa_ref[...], preferred_element_type=F32),
                jnp.dot(h, wb_ref[...], preferred_element_type=F32))

    nxt = up(rows[0])
    for s, r in enumerate(rows):
        a, b = nxt
        if s + 1 < len(rows):
            nxt = up(rows[s + 1])
        act = (_silu(a) * b).astype(BF16)
        y = jnp.dot(act, wo_ref[...], preferred_element_type=F32)
        z = ALPHA * x_ref[r, :] + (0.5 * gate) * y
        o_ref[r, :] = _layer_norm(z, g_ref[...], b_ref[...])


def _resident(shape, index_map):
    return pl.BlockSpec(shape, index_map, pipeline_mode=pl.Buffered(1))


def _ffn(x, mods, w_in, w_out, layer, half, ln_g, ln_b, sub, seq):
    t, d = x.shape
    dff = w_out.shape[2]
    tm = min(FFN_TM, seq)
    per_b = seq // tm
    return pl.pallas_call(
        functools.partial(_ffn_kernel, sub=sub),
        grid=(t // tm,),
        in_specs=[
            pl.BlockSpec((tm, d), lambda i: (i, 0)),
            pl.BlockSpec((1, 9, d), lambda i: (i // per_b, 0, 0)),
            _resident((None, None, d, dff), lambda i: (layer, half, 0, 0)),
            _resident((None, None, d, dff), lambda i: (layer, half, 0, 1)),
            _resident((None, None, dff, d), lambda i: (layer, half, 0, 0)),
            _resident((1, d), lambda i: (0, 0)),
            _resident((1, d), lambda i: (0, 0)),
        ],
        out_specs=pl.BlockSpec((tm, d), lambda i: (i, 0)),
        out_shape=jax.ShapeDtypeStruct((t, d), F32),
        compiler_params=_params(("parallel",)),
        name="ffn",
    )(x, mods, w_in, w_in, w_out, ln_g.reshape(1, d), ln_b.reshape(1, d))


def _mm_kernel(x_ref, m_ref, w_ref, o_ref):
    shift = m_ref[0, 3:4, :]
    scale = m_ref[0, 4:5, :]
    h = (x_ref[...] * (1.0 + scale) + shift).astype(BF16)
    for j in range(w_ref.shape[1] // MM_TN):
        cols = slice(j * MM_TN, (j + 1) * MM_TN)
        o_ref[:, cols] = jnp.dot(h, w_ref[:, cols], preferred_element_type=F32).astype(o_ref.dtype)


def _mm_hgrn_kernel(x_ref, m_ref, w_ref, lb_ref, o_ref):
    shift = m_ref[0, 3:4, :]
    scale = m_ref[0, 4:5, :]
    h = (x_ref[...] * (1.0 + scale) + shift).astype(BF16)
    for j in range(w_ref.shape[1] // MM_TN):
        cols = slice(j * MM_TN, (j + 1) * MM_TN)
        y = jnp.dot(h, w_ref[:, cols], preferred_element_type=F32)
        if j == 0:
            hq = 0.5 * y
            y = hq + hq * jnp.tanh(hq)
        elif j >= 2:
            lbv = lb_ref[j - 2]
            y = lbv + (1.0 - lbv) * jax.nn.sigmoid(y)
        o_ref[:, cols] = y


def _mod_matmul(x, mods, w, out_dtype, seq, lbs=None):
    t, d = x.shape
    n = w.shape[1]
    tm = min(MM_TM, seq)
    per_b = seq // tm
    in_specs = [
        pl.BlockSpec((tm, d), lambda i: (i, 0)),
        pl.BlockSpec((1, 9, d), lambda i: (i // per_b, 0, 0)),
        _resident((d, n), lambda i: (0, 0)),
    ]
    args = [x, mods, w]
    if lbs is not None:
        assert n == 4 * MM_TN and MM_TN == d
        in_specs.append(_resident((2, 1, d), lambda i: (0, 0, 0)))
        args.append(lbs.reshape(2, 1, d))
    return pl.pallas_call(
        _mm_kernel if lbs is None else _mm_hgrn_kernel,
        grid=(t // tm,),
        in_specs=in_specs,
        out_specs=pl.BlockSpec((tm, n), lambda i: (i, 0)),
        out_shape=jax.ShapeDtypeStruct((t, n), out_dtype),
        compiler_params=_params(("parallel",), 48 * 1024 * 1024),
        name="mm",
    )(*args)


def _mixer_epilogue(x_ref, m_ref, y, g_ref, b_ref, o_ref, rows=slice(None)):
    gate = m_ref[0, 5:6, :]
    z = ALPHA * x_ref[rows, :] + gate * y
    o_ref[rows, :] = _layer_norm(z, g_ref[...], b_ref[...])


def _sub_tiles(ref):
    half = ref.shape[0] // 2
    return (slice(0, half), slice(half, 2 * half))


def _out_hgrn_kernel(of_ref, ob_ref, x_ref, m_ref, ng_ref, wg_ref, w_ref, g_ref, b_ref, o_ref):
    tiles = _sub_tiles(x_ref)
    gates = []
    for r in tiles:
        h = (x_ref[r, :] * (1.0 + m_ref[0, 4:5, :]) + m_ref[0, 3:4, :]).astype(BF16)
        gates.append(jnp.dot(h, wg_ref[...], preferred_element_type=F32))
    ys = []
    for r, gate in zip(tiles, gates):
        o = of_ref[r, :] + ob_ref[r, :]
        parts = []
        for hd in range(HGRN_HEADS):
            oh = o[:, hd * HGRN_HEAD_DIM:(hd + 1) * HGRN_HEAD_DIM]
            ms = jnp.mean(oh * oh, axis=-1, keepdims=True)
            parts.append(oh * lax.rsqrt(ms + LN_EPS))
        on = jnp.concatenate(parts, axis=1)
        lhs = (on * ng_ref[...] * _silu(gate)).astype(BF16)
        ys.append(jnp.dot(lhs, w_ref[...], preferred_element_type=F32))
    for r, y in zip(tiles, ys):
        _mixer_epilogue(x_ref, m_ref, y, g_ref, b_ref, o_ref, r)


def _out_s5_kernel(a_ref, x_ref, m_ref, w_ref, g_ref, b_ref, o_ref):
    d = x_ref.shape[1]
    tiles = _sub_tiles(x_ref)
    yys = [jnp.dot(a_ref[r, :].astype(BF16), w_ref[...], preferred_element_type=F32) for r in tiles]
    for r, yy in zip(tiles, yys):
        y = yy[:, :d] * jax.nn.sigmoid(yy[:, d:])
        _mixer_epilogue(x_ref, m_ref, y, g_ref, b_ref, o_ref, r)


def _out_call(kernel, name, lead_specs, lead_args, x, mods, mid_specs, mid_args, w, ln_g, ln_b, seq):
    t, d = x.shape
    tm = min(OUT_TM, seq)
    per_b = seq // tm
    in_specs = list(lead_specs) + [
        pl.BlockSpec((tm, d), lambda i: (i, 0)),
        pl.BlockSpec((1, 9, d), lambda i: (i // per_b, 0, 0)),
    ] + list(mid_specs) + [
        _resident(w.shape, lambda i: (0, 0)),
        _resident((1, d), lambda i: (0, 0)),
        _resident((1, d), lambda i: (0, 0)),
    ]
    return pl.pallas_call(
        kernel,
        grid=(t // tm,),
        in_specs=in_specs,
        out_specs=pl.BlockSpec((tm, d), lambda i: (i, 0)),
        out_shape=jax.ShapeDtypeStruct((t, d), F32),
        compiler_params=_params(("parallel",), 40 * 1024 * 1024),
        name=name,
    )(*lead_args, x, mods, *mid_args, w, ln_g.reshape(1, d), ln_b.reshape(1, d))


def _na_bias_table(rpb):
    qc = np.arange(GRID_W)[:, None]
    kc = np.arange(GRID_W)[None, :]
    cs = np.clip(qc - NA_COLS // 2, 0, GRID_W - NA_COLS)
    mask = (kc >= cs) & (kc < cs + NA_COLS)
    dc = np.clip(kc - qc + NA_COLS - 1, 0, 2 * NA_COLS - 2)
    onehot = (dc.reshape(1, -1) == np.arange(2 * NA_COLS - 1)[:, None]).astype(np.float32)
    cols = jnp.dot(rpb.astype(F32).reshape(-1, 2 * NA_COLS - 1), jnp.asarray(onehot),
                   precision=lax.Precision.HIGHEST)
    cols = cols.reshape(NA_HEADS, 2 * NA_ROWS - 1, GRID_W, GRID_W)
    tbl = jnp.where(jnp.asarray(mask)[None, None], cols, -jnp.inf)
    tbl2 = jnp.concatenate([tbl[:, :-1], tbl[:, 1:]], axis=-1)
    return tbl2.reshape(NA_HEADS * (2 * NA_ROWS - 2), GRID_W, 2 * GRID_W)


def _na_kernel(q_ref, kp_ref, kc_ref, kn_ref, vp_ref, vc_ref, vn_ref, tbl_ref,
               x_ref, m_ref, wo_ref, g_ref, b_ref, o_ref, k_scr, v_scr, att_scr, *, rows):
    w = GRID_W
    half = NA_RB * w // 2
    full = NA_RB * w
    k_scr[0:half] = kp_ref[...]
    k_scr[half:half + full] = kc_ref[...]
    k_scr[half + full:] = kn_ref[...]
    v_scr[0:half] = vp_ref[...]
    v_scr[half:half + full] = vc_ref[...]
    v_scr[half + full:] = vn_ref[...]

    r0 = pl.program_id(1) * NA_RB
    lane = lax.broadcasted_iota(jnp.int32, (w, 2 * NA_HEAD_DIM), 1)
    lo = lane < NA_HEAD_DIM
    nd = 2 * NA_ROWS - 2

    def pair_slice(h):
        return slice(2 * NA_HEAD_DIM * (h // 2), 2 * NA_HEAD_DIM * (h // 2 + 1))

    def geometry(rr):
        r = r0 + rr
        rs = jnp.clip(r - NA_ROWS // 2, 0, rows - NA_ROWS)
        woff = pl.multiple_of((rs - r0 + NA_ROWS // 2) * w, w)
        return woff, r - rs, pl.multiple_of(rr * w, w)

    def score_pass(geo):
        woff, _, qoff = geo
        scores = []
        for hp in range(NA_HEADS // 2):
            sl = pair_slice(2 * hp)
            q2 = q_ref[pl.ds(qoff, w), sl]
            k2 = k_scr[pl.ds(woff, NA_ROWS * w), sl]
            zero = jnp.zeros_like(q2)
            qm = jnp.concatenate([jnp.where(lo, q2, zero), jnp.where(lo, zero, q2)], axis=0)
            scores.append(lax.dot_general(qm, k2, (((1,), (1,)), ((), ())), preferred_element_type=F32))
        return scores

    def softmax_value_pass(geo, scores):
        woff, rel, qoff = geo
        probs, denom = [], []
        for hp in range(NA_HEADS // 2):
            bias = jnp.concatenate(
                [jnp.concatenate([tbl_ref[h * nd + 2 * j + (NA_ROWS - 1) - rel] for j in range(NA_ROWS // 2)],
                                 axis=1) for h in (2 * hp, 2 * hp + 1)], axis=0)
            s = scores[hp] + bias
            m = jnp.max(s, axis=1, keepdims=True)
            p = jnp.exp(s - m)
            denom.append(jnp.sum(p, axis=1, keepdims=True))
            probs.append(p.astype(BF16))
        outs = []
        for hp in range(NA_HEADS // 2):
            v2 = v_scr[pl.ds(woff, NA_ROWS * w), pair_slice(2 * hp)]
            o = jnp.dot(probs[hp], v2, preferred_element_type=F32) / denom[hp]
            outs.append(jnp.where(lo, o[:w], o[w:]))
        att_scr[pl.ds(qoff, w), :] = jnp.concatenate(outs, axis=1).astype(att_scr.dtype)

    def rows_body(it, carry):
        geos = [geometry(it * NA_UNROLL + u) for u in range(NA_UNROLL)]
        scores = [score_pass(g) for g in geos]
        for g, s in zip(geos, scores):
            softmax_value_pass(g, s)
        return carry

    lax.fori_loop(0, NA_RB // NA_UNROLL, rows_body, 0)
    y = jnp.dot(att_scr[...], wo_ref[...], preferred_element_type=F32)
    _mixer_epilogue(x_ref, m_ref, y, g_ref, b_ref, o_ref)


def _na_core(qkv, tbl, x, mods, w_out, ln_g, ln_b, bsz, seq):
    t = qkv.shape[0]
    d = D_MODEL
    rows = seq // GRID_W
    nrb = rows // NA_RB
    full = NA_RB * GRID_W
    half = full // 2

    def cur(col):
        return pl.BlockSpec((full, d), lambda b, i: (b * nrb + i, col))

    def prev(col):
        return pl.BlockSpec((half, d), lambda b, i: (b * 2 * nrb + jnp.maximum(2 * i - 1, 0), col))

    def nxt(col):
        return pl.BlockSpec((half, d), lambda b, i: (b * 2 * nrb + jnp.minimum(2 * i + 2, 2 * nrb - 1), col))

    return pl.pallas_call(
        functools.partial(_na_kernel, rows=rows),
        grid=(bsz, nrb),
        in_specs=[cur(0), prev(1), cur(1), nxt(1), prev(2), cur(2), nxt(2),
                  _resident(tbl.shape, lambda b, i: (0, 0, 0)),
                  cur(0),
                  pl.BlockSpec((1, 9, d), lambda b, i: (b, 0, 0)),
                  _resident((d, d), lambda b, i: (0, 0)),
                  _resident((1, d), lambda b, i: (0, 0)),
                  _resident((1, d), lambda b, i: (0, 0))],
        out_specs=pl.BlockSpec((full, d), lambda b, i: (b * nrb + i, 0)),
        out_shape=jax.ShapeDtypeStruct((t, d), F32),
        scratch_shapes=[pltpu.VMEM((2 * full, d), BF16), pltpu.VMEM((2 * full, d), BF16),
                        pltpu.VMEM((full, d), BF16)],
        compiler_params=_params(("parallel", "arbitrary")),
        name="na",
    )(qkv, qkv, qkv, qkv, qkv, qkv, qkv, tbl, x, mods, w_out, ln_g.reshape(1, d), ln_b.reshape(1, d))


def _gla_kernel(qf_ref, vf_ref, ff_ref, qb_ref, vb_ref, fb_ref, of_ref, ob_ref, st_scr):
    c = HGRN_CHUNK
    dh = HGRN_HEAD_DIM
    ns = 2 * GLA_NB
    tdims = (((1,), (1,)), ((), ()))
    in_refs = ((qf_ref, vf_ref, ff_ref), (qb_ref, vb_ref, fb_ref))
    out_refs = (of_ref, ob_ref)

    @pl.when(pl.program_id(1) == 0)
    def _():
        st_scr[...] = jnp.zeros_like(st_scr)

    row = lax.broadcasted_iota(jnp.int32, (c, c), 0)
    col = lax.broadcasted_iota(jnp.int32, (c, c), 1)
    tris = (col <= row, col >= row)
    tri3 = [jnp.concatenate([t.astype(F32).astype(BF16)] * 3, axis=1) for t in tris]
    mid_row = (c // 2, c // 2 - 1)
    last_row = (c - 1, 0)

    def chunk_body(ci, carry):
        offs = (pl.multiple_of(ci * c, c), pl.multiple_of((GLA_CB - 1 - ci) * c, c))
        pre = []
        for s in range(ns):
            bb, dr = s // 2, s % 2
            q_ref, v_ref, f_ref = in_refs[dr]
            f = f_ref[bb, pl.ds(offs[dr], c), :]
            g = jnp.log(f)
            k = 1.0 - f
            g1 = g.astype(BF16)
            r1 = g - g1.astype(F32)
            g2 = r1.astype(BF16)
            g3 = (r1 - g2.astype(F32)).astype(BF16)
            bc = jnp.dot(tri3[dr], jnp.concatenate([g1, g2, g3], axis=0), preferred_element_type=F32)
            b_mid = bc[mid_row[dr]:mid_row[dr] + 1]
            b_last = bc[last_row[dr]:last_row[dr] + 1]
            qs = q_ref[bb, pl.ds(offs[dr], c), :]
            pre.append(dict(
                qe=(qs * jnp.exp(bc - b_mid)).astype(BF16),
                ke=(k * jnp.exp(b_mid - bc)).astype(BF16),
                qd=(qs * jnp.exp(bc)).astype(BF16),
                kd=(k * jnp.exp(b_last - bc)).astype(BF16),
                dec=jnp.exp(b_last),
                v=v_ref[bb, pl.ds(offs[dr], c), :].astype(BF16)))
        units = [(s, h) for s in range(ns) for h in range(HGRN_HEADS)]
        sls = [slice(h * dh, (h + 1) * dh) for h in range(HGRN_HEADS)]
        att = [lax.dot_general(pre[s]["qe"][:, sls[h]], pre[s]["ke"][:, sls[h]], tdims,
                               preferred_element_type=F32) for s, h in units]
        inter, upd = [], []
        for s, h in units:
            st = st_scr[s, sls[h], :]
            inter.append(lax.dot_general(pre[s]["qd"][:, sls[h]], st.astype(BF16), tdims,
                                         preferred_element_type=F32))
            new = lax.dot_general(pre[s]["v"][:, sls[h]], pre[s]["kd"][:, sls[h]],
                                  (((0,), (0,)), ((), ())), preferred_element_type=F32)
            upd.append(pre[s]["dec"][:, sls[h]] * st + new)
        for u, (s, h) in enumerate(units):
            st_scr[s, sls[h], :] = upd[u]
        for u, (s, h) in enumerate(units):
            a = jnp.where(tris[s % 2], att[u], 0.0).astype(BF16)
            o = jnp.dot(a, pre[s]["v"][:, sls[h]], preferred_element_type=F32) + inter[u]
            out_refs[s % 2][s // 2, pl.ds(offs[s % 2], c), sls[h]] = o
        return carry

    lax.fori_loop(0, GLA_CB, chunk_body, 0)


def _gla(proj, bsz, seq):
    t = proj.shape[0]
    d = D_MODEL
    tb = GLA_CB * HGRN_CHUNK
    nblk = seq // tb
    nb = min(GLA_NB, bsz)
    assert nb == GLA_NB and bsz % nb == 0
    proj3 = proj.reshape(bsz, seq, proj.shape[1])

    def fwd(col):
        return pl.BlockSpec((nb, tb, d), lambda b, i: (b, i, col))

    def bwd(col):
        return pl.BlockSpec((nb, tb, d), lambda b, i: (b, nblk - 1 - i, col))

    o_f, o_b = pl.pallas_call(
        _gla_kernel,
        grid=(bsz // nb, nblk),
        in_specs=[fwd(0), fwd(1), fwd(2), bwd(0), bwd(1), bwd(3)],
        out_specs=[fwd(0), bwd(0)],
        out_shape=[jax.ShapeDtypeStruct((bsz, seq, d), F32)] * 2,
        scratch_shapes=[pltpu.VMEM((2 * nb, d, HGRN_HEAD_DIM), F32)],
        compiler_params=_params(("parallel", "arbitrary"), 48 * 1024 * 1024),
        name="gla",
    )(proj3, proj3, proj3, proj3, proj3, proj3)
    return o_f.reshape(t, d), o_b.reshape(t, d)


def _gelu_tanh(y):
    return 0.5 * y * (1.0 + jnp.tanh(math.sqrt(2.0 / math.pi) * (y + 0.044715 * (y * y * y))))


def _s5n_coef_kernel(a_ref, bt_ref, c_ref, k2_ref, e8_ref, w8_ref, sc_ref, lam_ref):
    L, ch, p, R = S5_L, S5_GROUP_CH, S5_STATE, S5_R
    lanes = 2 * p
    hi = lax.Precision.HIGHEST
    e8_ref[...] = jnp.zeros_like(e8_ref)
    w8_ref[...] = jnp.zeros_like(w8_ref)
    for g8 in range(S5_LB):
        a = a_ref[g8]
        cr, ci = c_ref[g8, 0], c_ref[g8, 1]
        br, bi = bt_ref[g8, 0], bt_ref[g8, 1]
        zb, ang_dt = [], []
        for dr in range(2):
            ar = a[3 * dr:3 * dr + 1]
            ai = a[3 * dr + 1:3 * dr + 2]
            dt = jnp.exp(a[3 * dr + 2:3 * dr + 3])
            mag = jnp.exp(ar * dt)
            ang = ai * dt
            lr = mag * jnp.cos(ang)
            li = mag * jnp.sin(ang)
            den = ar * ar + ai * ai
            zr = ((lr - 1.0) * ar + li * ai) / den
            zi = (li * ar - (lr - 1.0) * ai) / den
            zb.append((zr * br - zi * bi, zr * bi + zi * br))
            ang_dt.append((ar * dt, ang))

        def powers(dr, m):
            lre, lang = ang_dt[dr]
            mg = jnp.exp(m * lre)
            return mg * jnp.cos(m * lang), mg * jnp.sin(m * lang)

        def outer(pw, coef, sign=1.0):
            pr, pi = pw
            fr, fi = coef
            r = pr.shape[0]
            re = pr[:, None, :] * fr[None] - pi[:, None, :] * fi[None]
            im = pr[:, None, :] * fi[None] + pi[:, None, :] * fr[None]
            return jnp.concatenate([re.reshape(r * ch, p), sign * im.reshape(r * ch, p)], axis=1)

        rows = slice(g8 * ch, (g8 + 1) * ch)
        n = lax.broadcasted_iota(jnp.int32, (2 * L, 1), 0)
        ccp = [jnp.zeros((g8 * ch, lanes), F32), jnp.concatenate([cr, -ci], axis=1),
               jnp.zeros(((S5_LB - 1 - g8) * ch, lanes), F32)]
        ccp = jnp.concatenate([v for v in ccp if v.shape[0]], axis=0)
        kn = jnp.zeros((2 * L * ch, S5_LB * ch), F32)
        for dr in range(2):
            lag = ((L - 1) - n) if dr == 0 else (n - (L - 1))
            valid = jnp.logical_and(lag >= 0, n <= 2 * L - 2).astype(F32)
            pr, pi = powers(dr, jnp.maximum(lag, 0).astype(F32))
            zbp = outer((pr * valid, pi * valid), zb[dr])
            kn = kn + lax.dot_general(zbp, ccp, (((1,), (1,)), ((), ())), precision=hi,
                                      preferred_element_type=F32)
        kn3 = kn.reshape(2 * L, ch, S5_LB * ch).astype(k2_ref.dtype)
        half = S5_LB * ch
        k2_ref[0, :, rows, 0:half] = kn3
        k2_ref[0, 1:, rows, half:] = kn3[:-1]
        k2_ref[0, 0:1, rows, half:] = jnp.zeros((1, ch, half), k2_ref.dtype)

        rr = lax.broadcasted_iota(jnp.int32, (R, 1), 0).astype(F32)
        pieces = (
            (e8_ref, outer(powers(0, (R - 1) - rr), zb[0]), outer(powers(1, rr), zb[1])),
            (w8_ref, outer(powers(0, rr + 1.0), (cr, ci), -1.0), outer(powers(1, R - rr), (cr, ci), -1.0)),
        )
        for ref, fwd, bwd in pieces:
            for dr, piece in enumerate((fwd, bwd)):
                col = (2 * g8 + dr) * lanes
                ref[0, :, rows, col:col + lanes] = piece.reshape(R, ch, lanes).astype(ref.dtype)

        lane = lax.broadcasted_iota(jnp.int32, (1, lanes), 1)
        sign = jnp.where(lane < p, -1.0, 1.0)

        def mul_rows(dr, m):
            pr, pi = powers(dr, jnp.full((1, 1), float(m), F32))
            return jnp.concatenate([pr, pr], axis=1), jnp.concatenate([pi, pi], axis=1) * sign

        na = L // R
        for a_i in range(na):
            for dr, m in ((0, R * a_i), (1, R * (na - 1 - a_i))):
                col = (2 * g8 + dr) * lanes
                ca, cb = mul_rows(dr, m)
                sc_ref[0, a_i:a_i + 1, col:col + lanes] = ca
                sc_ref[0, na + a_i:na + a_i + 1, col:col + lanes] = cb
        for dr in range(2):
            ca, cb = mul_rows(dr, L)
            lam_ref[0, 2 * dr, g8:g8 + 1, :] = ca
            lam_ref[0, 2 * dr + 1, g8:g8 + 1, :] = cb


def _s5n_coefs(a_re, a_im, log_dt, b_re, b_im, c_re, c_im):
    g, p, ch, L, R = S5_GROUPS, S5_STATE, S5_GROUP_CH, S5_L, S5_R
    nlb = g // S5_LB
    lw = S5_LB * ch
    sw = S5_LB * 4 * p
    ldt = jnp.broadcast_to(log_dt[:, :, None], (2, g, p))
    zero = jnp.zeros((g, p), F32)
    amat = jnp.stack([a_re[0], a_im[0], ldt[0], a_re[1], a_im[1], ldt[1], zero, zero], axis=1)
    bt = jnp.stack([b_re, b_im], axis=1).transpose(0, 1, 3, 2)
    cc = jnp.stack([c_re, c_im], axis=1)
    k2, e8, w8, sc, lam = pl.pallas_call(
        _s5n_coef_kernel,
        grid=(nlb,),
        in_specs=[
            pl.BlockSpec((S5_LB, 8, p), lambda i: (i, 0, 0)),
            pl.BlockSpec((S5_LB, 2, ch, p), lambda i: (i, 0, 0, 0)),
            pl.BlockSpec((S5_LB, 2, ch, p), lambda i: (i, 0, 0, 0)),
        ],
        out_specs=[
            pl.BlockSpec((1, 2 * L, lw, 2 * lw), lambda i: (i, 0, 0, 0)),
            pl.BlockSpec((1, R, lw, sw), lambda i: (i, 0, 0, 0)),
            pl.BlockSpec((1, R, lw, sw), lambda i: (i, 0, 0, 0)),
            pl.BlockSpec((1, 2 * (L // R), sw), lambda i: (i, 0, 0)),
            pl.BlockSpec((1, 4, S5_LB, 2 * p), lambda i: (i, 0, 0, 0)),
        ],
        out_shape=[
            jax.ShapeDtypeStruct((nlb, 2 * L, lw, 2 * lw), BF16),
            jax.ShapeDtypeStruct((nlb, R, lw, sw), BF16),
            jax.ShapeDtypeStruct((nlb, R, lw, sw), BF16),
            jax.ShapeDtypeStruct((nlb, 2 * (L // R), sw), F32),
            jax.ShapeDtypeStruct((nlb, 4, S5_LB, 2 * p), F32),
        ],
        compiler_params=_params(("parallel",), 48 * 1024 * 1024),
        name="s5_coef",
    )(amat, bt, cc)
    return k2.reshape(nlb, 2 * L * lw, 2 * lw), e8.reshape(nlb, R * lw, sw), w8.reshape(nlb, R * lw, sw), sc, lam


def _swap_halves(v):
    w = 2 * S5_STATE
    return jnp.concatenate([pltpu.roll(v[:, k * w:(k + 1) * w], S5_STATE, 1) for k in range(v.shape[1] // w)],
                           axis=1)


def _s5n_load_chunks(x_ref, m_ref, s):
    tm = x_ref.shape[0] // S5_L
    return x_ref[pl.ds(s, tm, stride=S5_L), :] * (1.0 + m_ref[0, 4:5, :]) + m_ref[0, 3:4, :]


def _s5n_in_kernel(x_ref, m_ref, e8_ref, sc_ref, e_ref):
    L, R, w = S5_L, S5_R, 2 * S5_STATE
    na = L // R
    tm = x_ref.shape[0] // L
    acc = jnp.zeros((tm, e8_ref.shape[2]), F32)
    for a in range(na):
        xa = jnp.concatenate([_s5n_load_chunks(x_ref, m_ref, R * a + r).astype(BF16) for r in range(R)], axis=1)
        ea = jnp.dot(xa, e8_ref[0], preferred_element_type=F32)
        ta = na - 1 - a
        acc = acc + sc_ref[0, ta:ta + 1, :] * ea + sc_ref[0, na + ta:na + ta + 1, :] * _swap_halves(ea)
    for plane, val in enumerate((acc, _swap_halves(acc))):
        for g8 in range(S5_LB):
            for dr in range(2):
                col = (2 * g8 + dr) * w
                e_ref[0, dr, plane, 0, pl.ds(g8, tm, stride=S5_LB), :] = val[:, col:col + w]


def _s5n_scan_kernel(e_ref, lam_ref, s_ref):
    nb = e_ref.shape[3]
    sub = S5_LB
    nch = e_ref.shape[4] // sub
    dr = pl.program_id(1)
    ca, cb = lam_ref[0, 0], lam_ref[0, 1]

    def body(j, carry):
        off = pl.multiple_of(jnp.where(dr == 0, j, nch - 1 - j) * sub, sub)
        new = []
        for b in range(nb):
            x, xs = carry[2 * b], carry[2 * b + 1]
            s_ref[0, 0, b, pl.ds(off, sub), :] = x
            new.append(ca * x + cb * xs + e_ref[0, 0, 0, b, pl.ds(off, sub), :])
            new.append(ca * xs - cb * x + e_ref[0, 0, 1, b, pl.ds(off, sub), :])
        return tuple(new)

    zero = jnp.zeros((sub, 2 * S5_STATE), F32)
    lax.fori_loop(0, nch, body, tuple([zero] * (2 * nb)))


def _s5n_out_kernel(x_ref, m_ref, d_ref, k2_ref, s_ref, w8_ref, sc_ref, y_ref):
    L, R, w = S5_L, S5_R, 2 * S5_STATE
    na = L // R
    lw = x_ref.shape[1]
    tm = x_ref.shape[0] // L
    xs = [_s5n_load_chunks(x_ref, m_ref, s) for s in range(L)]
    xflat = jnp.concatenate([v.astype(BF16) for v in xs], axis=1)
    st = jnp.concatenate([s_ref[0, dr, 0, pl.ds(g8, tm, stride=S5_LB), :]
                          for g8 in range(S5_LB) for dr in range(2)], axis=1)
    stsw = _swap_halves(st)
    corr = []
    for a in range(na):
        sa = (sc_ref[0, a:a + 1, :] * st + sc_ref[0, na + a:na + a + 1, :] * stsw).astype(BF16)
        corr.append(lax.dot_general(sa, w8_ref[0], (((1,), (1,)), ((), ())), preferred_element_type=F32))
    for tp in range(L // 2):
        start = (L - 1 - 2 * tp) * lw
        yp = jnp.dot(xflat, k2_ref[0, start:start + L * lw, :], preferred_element_type=F32)
        for q in range(2):
            t = 2 * tp + q
            y = yp[:, q * lw:(q + 1) * lw] + corr[t // R][:, (t % R) * lw:(t % R + 1) * lw] + d_ref[...] * xs[t]
            y_ref[pl.ds(t, tm, stride=L), :] = _gelu_tanh(y)


def _s5n_core(x, mods, coefs, d_skip, bsz, seq):
    k2, e8, w8, sc, lam = coefs
    t, d = x.shape
    L, p = S5_L, S5_STATE
    lw = S5_LB * S5_GROUP_CH
    nlb = d // lw
    mb = seq // L
    rows = mb * S5_LB

    x_spec = pl.BlockSpec((seq, lw), lambda gi, b: (b, gi))
    m_spec = pl.BlockSpec((1, 9, lw), lambda gi, b: (b, 0, gi))
    st_spec = pl.BlockSpec((1, 2, 1, rows, 2 * p), lambda gi, b: (gi, 0, b, 0, 0))
    st_shape = jax.ShapeDtypeStruct((nlb, 2, bsz, rows, 2 * p), F32)

    def per_block(arr):
        return pl.BlockSpec((1,) + arr.shape[1:], lambda gi, b: (gi,) + (0,) * (arr.ndim - 1),
                            pipeline_mode=pl.Buffered(1))

    e = pl.pallas_call(
        _s5n_in_kernel,
        grid=(nlb, bsz),
        in_specs=[x_spec, m_spec, per_block(e8), per_block(sc)],
        out_specs=pl.BlockSpec((1, 2, 2, 1, rows, 2 * p), lambda gi, b: (gi, 0, 0, b, 0, 0)),
        out_shape=jax.ShapeDtypeStruct((nlb, 2, 2, bsz, rows, 2 * p), F32),
        compiler_params=_params(("parallel", "parallel"), 40 * 1024 * 1024),
        name="s5_in",
    )(x, mods, e8, sc)

    states = pl.pallas_call(
        _s5n_scan_kernel,
        grid=(nlb, 2),
        in_specs=[
            pl.BlockSpec((1, 1, 2, bsz, rows, 2 * p), lambda gi, dr: (gi, dr, 0, 0, 0, 0)),
            pl.BlockSpec((1, 2, S5_LB, 2 * p), lambda gi, dr: (gi, dr, 0, 0)),
        ],
        out_specs=pl.BlockSpec((1, 1, bsz, rows, 2 * p), lambda gi, dr: (gi, dr, 0, 0, 0)),
        out_shape=st_shape,
        compiler_params=_params(("parallel", "parallel")),
        name="s5_scan",
    )(e, lam)

    return pl.pallas_call(
        _s5n_out_kernel,
        grid=(nlb, bsz),
        in_specs=[x_spec, m_spec, pl.BlockSpec((1, lw), lambda gi, b: (0, gi)),
                  per_block(k2), st_spec, per_block(w8), per_block(sc)],
        out_specs=x_spec,
        out_shape=jax.ShapeDtypeStruct((t, d), F32),
        compiler_params=_params(("parallel", "parallel"), 52 * 1024 * 1024),
        name="s5_out",
    )(x, mods, d_skip.reshape(1, d), k2, states, w8, sc)


def _row_spec(tm, d, col=0):
    return pl.BlockSpec((tm, d), lambda i: (i, col))


def _na_mixer(x, mods, w_qkv, rpb, w_out, ln_g, ln_b, bsz, seq):
    d = D_MODEL
    scale = jnp.concatenate([jnp.full((d,), NA_HEAD_DIM ** -0.5, F32), jnp.ones((2 * d,), F32)])
    qkv = _mod_matmul(x, mods, (w_qkv * scale).astype(BF16), BF16, seq)
    return _na_core(qkv, _na_bias_table(rpb), x, mods, w_out.astype(BF16), ln_g, ln_b, bsz, seq)


def _hgrn_mixer(x, mods, w_in, lbs, norm_g, w_out, ln_g, ln_b, bsz, seq):
    d = D_MODEL
    w_bf = w_in.astype(BF16)
    proj = _mod_matmul(x, mods, w_bf[:, :4 * d], F32, seq, lbs=lbs)
    o_f, o_b = _gla(proj, bsz, seq)
    tm = min(OUT_TM, seq)
    return _out_call(_out_hgrn_kernel, "out_hgrn", [_row_spec(tm, d), _row_spec(tm, d)], [o_f, o_b], x, mods,
                     [_resident((1, d), lambda i: (0, 0)), _resident((d, d), lambda i: (0, 0))],
                     [norm_g.reshape(1, d), w_bf[:, 4 * d:]],
                     w_out.astype(BF16), ln_g, ln_b, seq)


def _s5_mixer(x, mods, coefs, d_skip, w_out, ln_g, ln_b, bsz, seq):
    d = D_MODEL
    act = _s5n_core(x, mods, coefs, d_skip, bsz, seq)
    tm = min(OUT_TM, seq)
    return _out_call(_out_s5_kernel, "out_s5", [_row_spec(tm, d)], [act], x, mods, [], [],
                     w_out.astype(BF16), ln_g, ln_b, seq)


def _hgrn_lower_bounds(lb_logits):
    p = jax.nn.softmax(lb_logits.astype(F32), axis=1)
    cs = jnp.cumsum(p, axis=1)
    return cs - cs[:, :1]


def kernel(x, c, ada_w, ada_b, ln_g, ln_b, ffn_w_in, ffn_w_out, na_w_qkv, na_rpb, na_w_out,
           hgrn_w_in, hgrn_lb_logits, hgrn_norm_g, hgrn_w_out,
           s5_a_re, s5_a_im, s5_log_dt, s5_b_re, s5_b_im, s5_c_re, s5_c_im, s5_d, s5_w_out):
    bsz, seq, d = x.shape
    depth = ada_w.shape[0]
    mods_all = _ada_mods(c, ada_w, ada_b)
    lbs = _hgrn_lower_bounds(hgrn_lb_logits)
    xf = x.reshape(bsz * seq, d)
    w_in_bf = ffn_w_in.astype(BF16)
    w_out_bf = ffn_w_out.astype(BF16)
    for i in range(depth):
        kind = i % N_MIXERS
        j = i // N_MIXERS
        mods = mods_all[i]
        xf = _ffn(xf, mods, w_in_bf, w_out_bf, i, 0, ln_g[i, 0], ln_b[i, 0], 0, seq)
        if kind == 0:
            xf = _na_mixer(xf, mods, na_w_qkv[j], na_rpb[j], na_w_out[j], ln_g[i, 1], ln_b[i, 1], bsz, seq)
        elif kind == 1:
            xf = _hgrn_mixer(xf, mods, hgrn_w_in[j], lbs[:, i], hgrn_norm_g[j], hgrn_w_out[j],
                             ln_g[i, 1], ln_b[i, 1], bsz, seq)
        else:
            coefs = _s5n_coefs(s5_a_re[j], s5_a_im[j], s5_log_dt[j], s5_b_re[j], s5_b_im[j],
                               s5_c_re[j], s5_c_im[j])
            xf = _s5_mixer(xf, mods, coefs, s5_d[j], s5_w_out[j], ln_g[i, 1], ln_b[i, 1], bsz, seq)
        xf = _ffn(xf, mods, w_in_bf, w_out_bf, i, 1, ln_g[i, 2], ln_b[i, 2], 2, seq)
    return xf.reshape(bsz, seq, d)
```

```python
import functools
import math

import numpy as np
import jax
import jax.numpy as jnp
from jax import lax
from jax.experimental import pallas as pl
from jax.experimental.pallas import tpu as pltpu

F32 = jnp.float32
BF16 = jnp.bfloat16

D_MODEL = 1024
DEPTH = 4
N_MIXERS = 3
D_FF = 2816
LN_EPS = 1e-5
ALPHA = (2 * DEPTH) ** 0.25
GRID_W = 64
NA_HEADS = 16
NA_HEAD_DIM = D_MODEL // NA_HEADS
NA_ROWS = 8
NA_COLS = 16
HGRN_HEAD_DIM = 128
HGRN_HEADS = D_MODEL // HGRN_HEAD_DIM
HGRN_CHUNK = 64
S5_GROUP_CH = 16
S5_STATE = 64
S5_GROUPS = D_MODEL // S5_GROUP_CH

V7X_VMEM_BYTES = 64 * 1024 * 1024
VMEM_LIMIT = 56 * 1024 * 1024
FFN_TM = 1024
FFN_SPLIT = 4
MM_TM = 512
MM_TN = 1024
OUT_TM = 512
ADA_TN = 1152
NA_RB = 8
NA_UNROLL = 8
GLA_CB = 4
GLA_NB = 2
S5_L = 16
S5_R = 8
S5_LB = 8


def _params(sem, limit=VMEM_LIMIT):
    return pltpu.CompilerParams(dimension_semantics=sem, vmem_limit_bytes=limit)


def _layer_norm(z, g, b):
    mu = jnp.mean(z, axis=-1, keepdims=True)
    zc = z - mu
    var = jnp.mean(zc * zc, axis=-1, keepdims=True)
    return zc * lax.rsqrt(var + LN_EPS) * g + b


def _silu(a):
    return a * jax.nn.sigmoid(a)


def _ada_kernel(c_ref, w_ref, b_ref, o_ref):
    c = c_ref[...]
    sc = _silu(c).astype(BF16)
    w = w_ref[0].astype(BF16)
    o_ref[0] = jnp.dot(sc, w, preferred_element_type=F32) + b_ref[0]


def _ada_mods(c, ada_w, ada_b):
    bsz = c.shape[0]
    depth, d, n = ada_w.shape
    rows = 8
    cp = jnp.zeros((rows, d), F32).at[:bsz].set(c)
    out = pl.pallas_call(
        _ada_kernel,
        grid=(depth, n // ADA_TN),
        in_specs=[
            pl.BlockSpec((rows, d), lambda l, j: (0, 0)),
            pl.BlockSpec((1, d, ADA_TN), lambda l, j: (l, 0, j)),
            pl.BlockSpec((1, 1, ADA_TN), lambda l, j: (l, 0, j)),
        ],
        out_specs=pl.BlockSpec((1, rows, ADA_TN), lambda l, j: (l, 0, j)),
        out_shape=jax.ShapeDtypeStruct((depth, rows, n), F32),
        compiler_params=_params(("arbitrary", "arbitrary"), 32 * 1024 * 1024),
        name="ada",
    )(cp, ada_w, ada_b.reshape(depth, 1, n))
    return out[:, :bsz].reshape(depth, bsz, 9, d)


def _ffn_kernel(x_ref, m_ref, wa_ref, wb_ref, wo_ref, g_ref, b_ref, o_ref, *, sub):
    shift = m_ref[0, 3 * sub:3 * sub + 1, :]
    scale = m_ref[0, 3 * sub + 1:3 * sub + 2, :]
    gate = m_ref[0, 3 * sub + 2:3 * sub + 3, :]
    ts = x_ref.shape[0] // FFN_SPLIT
    rows = [slice(s * ts, (s + 1) * ts) for s in range(FFN_SPLIT)]
    def up(r):
        h = (x_ref[r, :] * (1.0 + scale) + shift).astype(BF16)
        return (jnp.dot(h, wa_ref[...], preferred_element_type=F32),
                jnp.dot(h, wb_ref[...], preferred_element_type=F32))

    nxt = up(rows[0])
    for s, r in enumerate(rows):
        a, b = nxt
        if s + 1 < len(rows):
            nxt = up(rows[s + 1])
        act = (_silu(a) * b).astype(BF16)
        y = jnp.dot(act, wo_ref[...], preferred_element_type=F32)
        z = ALPHA * x_ref[r, :] + (0.5 * gate) * y
        o_ref[r, :] = _layer_norm(z, g_ref[...], b_ref[...])


def _resident(shape, index_map):
    return pl.BlockSpec(shape, index_map, pipeline_mode=pl.Buffered(1))


def _ffn(x, mods, w_in, w_out, layer, half, ln_g, ln_b, sub, seq):
    t, d = x.shape
    dff = w_out.shape[2]
    tm = min(FFN_TM, seq)
    per_b = seq // tm
    return pl.pallas_call(
        functools.partial(_ffn_kernel, sub=sub),
        grid=(t // tm,),
        in_specs=[
            pl.BlockSpec((tm, d), lambda i: (i, 0)),
            pl.BlockSpec((1, 9, d), lambda i: (i // per_b, 0, 0)),
            _resident((None, None, d, dff), lambda i: (layer, half, 0, 0)),
            _resident((None, None, d, dff), lambda i: (layer, half, 0, 1)),
            _resident((None, None, dff, d), lambda i: (layer, half, 0, 0)),
            _resident((1, d), lambda i: (0, 0)),
            _resident((1, d), lambda i: (0, 0)),
        ],
        out_specs=pl.BlockSpec((tm, d), lambda i: (i, 0)),
        out_shape=jax.ShapeDtypeStruct((t, d), F32),
        compiler_params=_params(("parallel",)),
        name="ffn",
    )(x, mods, w_in, w_in, w_out, ln_g.reshape(1, d), ln_b.reshape(1, d))


def _mm_kernel(x_ref, m_ref, w_ref, o_ref):
    shift = m_ref[0, 3:4, :]
    scale = m_ref[0, 4:5, :]
    h = (x_ref[...] * (1.0 + scale) + shift).astype(BF16)
    for j in range(w_ref.shape[1] // MM_TN):
        cols = slice(j * MM_TN, (j + 1) * MM_TN)
        o_ref[:, cols] = jnp.dot(h, w_ref[:, cols], preferred_element_type=F32).astype(o_ref.dtype)


def _mm_hgrn_kernel(x_ref, m_ref, w_ref, lb_ref, o_ref):
    shift = m_ref[0, 3:4, :]
    scale = m_ref[0, 4:5, :]
    h = (x_ref[...] * (1.0 + scale) + shift).astype(BF16)
    for j in range(w_ref.shape[1] // MM_TN):
        cols = slice(j * MM_TN, (j + 1) * MM_TN)
        y = jnp.dot(h, w_ref[:, cols], preferred_element_type=F32)
        if j == 0:
            hq = 0.5 * y
            y = hq + hq * jnp.tanh(hq)
        elif j >= 2:
            lbv = lb_ref[j - 2]
            y = lbv + (1.0 - lbv) * jax.nn.sigmoid(y)
        o_ref[:, cols] = y


def _mod_matmul(x, mods, w, out_dtype, seq, lbs=None):
    t, d = x.shape
    n = w.shape[1]
    tm = min(MM_TM, seq)
    per_b = seq // tm
    in_specs = [
        pl.BlockSpec((tm, d), lambda i: (i, 0)),
        pl.BlockSpec((1, 9, d), lambda i: (i // per_b, 0, 0)),
        _resident((d, n), lambda i: (0, 0)),
    ]
    args = [x, mods, w]
    if lbs is not None:
        assert n == 4 * MM_TN and MM_TN == d
        in_specs.append(_resident((2, 1, d), lambda i: (0, 0, 0)))
        args.append(lbs.reshape(2, 1, d))
    return pl.pallas_call(
        _mm_kernel if lbs is None else _mm_hgrn_kernel,
        grid=(t // tm,),
        in_specs=in_specs,
        out_specs=pl.BlockSpec((tm, n), lambda i: (i, 0)),
        out_shape=jax.ShapeDtypeStruct((t, n), out_dtype),
        compiler_params=_params(("parallel",), 48 * 1024 * 1024),
        name="mm",
    )(*args)


def _mixer_epilogue(x_ref, m_ref, y, g_ref, b_ref, o_ref, rows=slice(None)):
    gate = m_ref[0, 5:6, :]
    z = ALPHA * x_ref[rows, :] + gate * y
    o_ref[rows, :] = _layer_norm(z, g_ref[...], b_ref[...])


def _sub_tiles(ref):
    half = ref.shape[0] // 2
    return (slice(0, half), slice(half, 2 * half))


def _out_hgrn_kernel(of_ref, ob_ref, x_ref, m_ref, ng_ref, wg_ref, w_ref, g_ref, b_ref, o_ref):
    tiles = _sub_tiles(x_ref)
    gates = []
    for r in tiles:
        h = (x_ref[r, :] * (1.0 + m_ref[0, 4:5, :]) + m_ref[0, 3:4, :]).astype(BF16)
        gates.append(jnp.dot(h, wg_ref[...], preferred_element_type=F32))
    ys = []
    for r, gate in zip(tiles, gates):
        o = of_ref[r, :] + ob_ref[r, :]
        parts = []
        for hd in range(HGRN_HEADS):
            oh = o[:, hd * HGRN_HEAD_DIM:(hd + 1) * HGRN_HEAD_DIM]
            ms = jnp.mean(oh * oh, axis=-1, keepdims=True)
            parts.append(oh * lax.rsqrt(ms + LN_EPS))
        on = jnp.concatenate(parts, axis=1)
        lhs = (on * ng_ref[...] * _silu(gate)).astype(BF16)
        ys.append(jnp.dot(lhs, w_ref[...], preferred_element_type=F32))
    for r, y in zip(tiles, ys):
        _mixer_epilogue(x_ref, m_ref, y, g_ref, b_ref, o_ref, r)


def _out_s5_kernel(a_ref, x_ref, m_ref, w_ref, g_ref, b_ref, o_ref):
    d = x_ref.shape[1]
    tiles = _sub_tiles(x_ref)
    yys = [jnp.dot(a_ref[r, :].astype(BF16), w_ref[...], preferred_element_type=F32) for r in tiles]
    for r, yy in zip(tiles, yys):
        y = yy[:, :d] * jax.nn.sigmoid(yy[:, d:])
        _mixer_epilogue(x_ref, m_ref, y, g_ref, b_ref, o_ref, r)


def _out_call(kernel, name, lead_specs, lead_args, x, mods, mid_specs, mid_args, w, ln_g, ln_b, seq):
    t, d = x.shape
    tm = min(OUT_TM, seq)
    per_b = seq // tm
    in_specs = list(lead_specs) + [
        pl.BlockSpec((tm, d), lambda i: (i, 0)),
        pl.BlockSpec((1, 9, d), lambda i: (i // per_b, 0, 0)),
    ] + list(mid_specs) + [
        _resident(w.shape, lambda i: (0, 0)),
        _resident((1, d), lambda i: (0, 0)),
        _resident((1, d), lambda i: (0, 0)),
    ]
    return pl.pallas_call(
        kernel,
        grid=(t // tm,),
        in_specs=in_specs,
        out_specs=pl.BlockSpec((tm, d), lambda i: (i, 0)),
        out_shape=jax.ShapeDtypeStruct((t, d), F32),
        compiler_params=_params(("parallel",), 40 * 1024 * 1024),
        name=name,
    )(*lead_args, x, mods, *mid_args, w, ln_g.reshape(1, d), ln_b.reshape(1, d))


def _na_bias_table(rpb):
    qc = np.arange(GRID_W)[:, None]
    kc = np.arange(GRID_W)[None, :]
    cs = np.clip(qc - NA_COLS // 2, 0, GRID_W - NA_COLS)
    mask = (kc >= cs) & (kc < cs + NA_COLS)
    dc = np.clip(kc - qc + NA_COLS - 1, 0, 2 * NA_COLS - 2)
    onehot = (dc.reshape(1, -1) == np.arange(2 * NA_COLS - 1)[:, None]).astype(np.float32)
    cols = jnp.dot(rpb.astype(F32).reshape(-1, 2 * NA_COLS - 1), jnp.asarray(onehot),
                   precision=lax.Precision.HIGHEST)
    cols = cols.reshape(NA_HEADS, 2 * NA_ROWS - 1, GRID_W, GRID_W)
    tbl = jnp.where(jnp.asarray(mask)[None, None], cols, -jnp.inf)
    tbl2 = jnp.concatenate([tbl[:, :-1], tbl[:, 1:]], axis=-1)
    return tbl2.reshape(NA_HEADS * (2 * NA_ROWS - 2), GRID_W, 2 * GRID_W)


def _na_kernel(q_ref, kp_ref, kc_ref, kn_ref, vp_ref, vc_ref, vn_ref, tbl_ref,
               x_ref, m_ref, wo_ref, g_ref, b_ref, o_ref, k_scr, v_scr, att_scr, *, rows):
    w = GRID_W
    half = NA_RB * w // 2
    full = NA_RB * w
    k_scr[0:half] = kp_ref[...]
    k_scr[half:half + full] = kc_ref[...]
    k_scr[half + full:] = kn_ref[...]
    v_scr[0:half] = vp_ref[...]
    v_scr[half:half + full] = vc_ref[...]
    v_scr[half + full:] = vn_ref[...]

    r0 = pl.program_id(1) * NA_RB
    lane = lax.broadcasted_iota(jnp.int32, (w, 2 * NA_HEAD_DIM), 1)
    lo = lane < NA_HEAD_DIM
    nd = 2 * NA_ROWS - 2

    def pair_slice(h):
        return slice(2 * NA_HEAD_DIM * (h // 2), 2 * NA_HEAD_DIM * (h // 2 + 1))

    def geometry(rr):
        r = r0 + rr
        rs = jnp.clip(r - NA_ROWS // 2, 0, rows - NA_ROWS)
        woff = pl.multiple_of((rs - r0 + NA_ROWS // 2) * w, w)
        return woff, r - rs, pl.multiple_of(rr * w, w)

    def score_pass(geo):
        woff, _, qoff = geo
        scores = []
        for hp in range(NA_HEADS // 2):
            sl = pair_slice(2 * hp)
            q2 = q_ref[pl.ds(qoff, w), sl]
            k2 = k_scr[pl.ds(woff, NA_ROWS * w), sl]
            zero = jnp.zeros_like(q2)
            qm = jnp.concatenate([jnp.where(lo, q2, zero), jnp.where(lo, zero, q2)], axis=0)
            scores.append(lax.dot_general(qm, k2, (((1,), (1,)), ((), ())), preferred_element_type=F32))
        return scores

    def softmax_value_pass(geo, scores):
        woff, rel, qoff = geo
        probs, denom = [], []
        for hp in range(NA_HEADS // 2):
            bias = jnp.concatenate(
                [jnp.concatenate([tbl_ref[h * nd + 2 * j + (NA_ROWS - 1) - rel] for j in range(NA_ROWS // 2)],
                                 axis=1) for h in (2 * hp, 2 * hp + 1)], axis=0)
            s = scores[hp] + bias
            m = jnp.max(s, axis=1, keepdims=True)
            p = jnp.exp(s - m)
            denom.append(jnp.sum(p, axis=1, keepdims=True))
            probs.append(p.astype(BF16))
        outs = []
        for hp in range(NA_HEADS // 2):
            v2 = v_scr[pl.ds(woff, NA_ROWS * w), pair_slice(2 * hp)]
            o = jnp.dot(probs[hp], v2, preferred_element_type=F32) / denom[hp]
            outs.append(jnp.where(lo, o[:w], o[w:]))
        att_scr[pl.ds(qoff, w), :] = jnp.concatenate(outs, axis=1).astype(att_scr.dtype)

    def rows_body(it, carry):
        geos = [geometry(it * NA_UNROLL + u) for u in range(NA_UNROLL)]
        scores = [score_pass(g) for g in geos]
        for g, s in zip(geos, scores):
            softmax_value_pass(g, s)
        return carry

    lax.fori_loop(0, NA_RB // NA_UNROLL, rows_body, 0)
    y = jnp.dot(att_scr[...], wo_ref[...], preferred_element_type=F32)
    _mixer_epilogue(x_ref, m_ref, y, g_ref, b_ref, o_ref)


def _na_core(qkv, tbl, x, mods, w_out, ln_g, ln_b, bsz, seq):
    t = qkv.shape[0]
    d = D_MODEL
    rows = seq // GRID_W
    nrb = rows // NA_RB
    full = NA_RB * GRID_W
    half = full // 2

    def cur(col):
        return pl.BlockSpec((full, d), lambda b, i: (b * nrb + i, col))

    def prev(col):
        return pl.BlockSpec((half, d), lambda b, i: (b * 2 * nrb + jnp.maximum(2 * i - 1, 0), col))

    def nxt(col):
        return pl.BlockSpec((half, d), lambda b, i: (b * 2 * nrb + jnp.minimum(2 * i + 2, 2 * nrb - 1), col))

    return pl.pallas_call(
        functools.partial(_na_kernel, rows=rows),
        grid=(bsz, nrb),
        in_specs=[cur(0), prev(1), cur(1), nxt(1), prev(2), cur(2), nxt(2),
                  _resident(tbl.shape, lambda b, i: (0, 0, 0)),
                  cur(0),
                  pl.BlockSpec((1, 9, d), lambda b, i: (b, 0, 0)),
                  _resident((d, d), lambda b, i: (0, 0)),
                  _resident((1, d), lambda b, i: (0, 0)),
                  _resident((1, d), lambda b, i: (0, 0))],
        out_specs=pl.BlockSpec((full, d), lambda b, i: (b * nrb + i, 0)),
        out_shape=jax.ShapeDtypeStruct((t, d), F32),
        scratch_shapes=[pltpu.VMEM((2 * full, d), BF16), pltpu.VMEM((2 * full, d), BF16),
                        pltpu.VMEM((full, d), BF16)],
        compiler_params=_params(("parallel", "arbitrary")),
        name="na",
    )(qkv, qkv, qkv, qkv, qkv, qkv, qkv, tbl, x, mods, w_out, ln_g.reshape(1, d), ln_b.reshape(1, d))


def _gla_kernel(qf_ref, vf_ref, ff_ref, qb_ref, vb_ref, fb_ref, of_ref, ob_ref, st_scr):
    c = HGRN_CHUNK
    dh = HGRN_HEAD_DIM
    ns = 2 * GLA_NB
    tdims = (((1,), (1,)), ((), ()))
    in_refs = ((qf_ref, vf_ref, ff_ref), (qb_ref, vb_ref, fb_ref))
    out_refs = (of_ref, ob_ref)

    @pl.when(pl.program_id(1) == 0)
    def _():
        st_scr[...] = jnp.zeros_like(st_scr)

    row = lax.broadcasted_iota(jnp.int32, (c, c), 0)
    col = lax.broadcasted_iota(jnp.int32, (c, c), 1)
    tris = (col <= row, col >= row)
    tri3 = [jnp.concatenate([t.astype(F32).astype(BF16)] * 3, axis=1) for t in tris]
    mid_row = (c // 2, c // 2 - 1)
    last_row = (c - 1, 0)

    def chunk_body(ci, carry):
        offs = (pl.multiple_of(ci * c, c), pl.multiple_of((GLA_CB - 1 - ci) * c, c))
        pre = []
        for s in range(ns):
            bb, dr = s // 2, s % 2
            q_ref, v_ref, f_ref = in_refs[dr]
            f = f_ref[bb, pl.ds(offs[dr], c), :]
            g = jnp.log(f)
            k = 1.0 - f
            g1 = g.astype(BF16)
            r1 = g - g1.astype(F32)
            g2 = r1.astype(BF16)
            g3 = (r1 - g2.astype(F32)).astype(BF16)
            bc = jnp.dot(tri3[dr], jnp.concatenate([g1, g2, g3], axis=0), preferred_element_type=F32)
            b_mid = bc[mid_row[dr]:mid_row[dr] + 1]
            b_last = bc[last_row[dr]:last_row[dr] + 1]
            qs = q_ref[bb, pl.ds(offs[dr], c), :]
            pre.append(dict(
                qe=(qs * jnp.exp(bc - b_mid)).astype(BF16),
                ke=(k * jnp.exp(b_mid - bc)).astype(BF16),
                qd=(qs * jnp.exp(bc)).astype(BF16),
                kd=(k * jnp.exp(b_last - bc)).astype(BF16),
                dec=jnp.exp(b_last),
                v=v_ref[bb, pl.ds(offs[dr], c), :].astype(BF16)))
        units = [(s, h) for s in range(ns) for h in range(HGRN_HEADS)]
        sls = [slice(h * dh, (h + 1) * dh) for h in range(HGRN_HEADS)]
        att = [lax.dot_general(pre[s]["qe"][:, sls[h]], pre[s]["ke"][:, sls[h]], tdims,
                               preferred_element_type=F32) for s, h in units]
        inter, upd = [], []
        for s, h in units:
            st = st_scr[s, sls[h], :]
            inter.append(lax.dot_general(pre[s]["qd"][:, sls[h]], st.astype(BF16), tdims,
                                         preferred_element_type=F32))
            new = lax.dot_general(pre[s]["v"][:, sls[h]], pre[s]["kd"][:, sls[h]],
                                  (((0,), (0,)), ((), ())), preferred_element_type=F32)
            upd.append(pre[s]["dec"][:, sls[h]] * st + new)
        for u, (s, h) in enumerate(units):
            st_scr[s, sls[h], :] = upd[u]
        for u, (s, h) in enumerate(units):
            a = jnp.where(tris[s % 2], att[u], 0.0).astype(BF16)
            o = jnp.dot(a, pre[s]["v"][:, sls[h]], preferred_element_type=F32) + inter[u]
            out_refs[s % 2][s // 2, pl.ds(offs[s % 2], c), sls[h]] = o
        return carry

    lax.fori_loop(0, GLA_CB, chunk_body, 0)


def _gla(proj, bsz, seq):
    t = proj.shape[0]
    d = D_MODEL
    tb = GLA_CB * HGRN_CHUNK
    nblk = seq // tb
    nb = min(GLA_NB, bsz)
    assert nb == GLA_NB and bsz % nb == 0
    proj3 = proj.reshape(bsz, seq, proj.shape[1])

    def fwd(col):
        return pl.BlockSpec((nb, tb, d), lambda b, i: (b, i, col))

    def bwd(col):
        return pl.BlockSpec((nb, tb, d), lambda b, i: (b, nblk - 1 - i, col))

    o_f, o_b = pl.pallas_call(
        _gla_kernel,
        grid=(bsz // nb, nblk),
        in_specs=[fwd(0), fwd(1), fwd(2), bwd(0), bwd(1), bwd(3)],
        out_specs=[fwd(0), bwd(0)],
        out_shape=[jax.ShapeDtypeStruct((bsz, seq, d), F32)] * 2,
        scratch_shapes=[pltpu.VMEM((2 * nb, d, HGRN_HEAD_DIM), F32)],
        compiler_params=_params(("parallel", "arbitrary"), 48 * 1024 * 1024),
        name="gla",
    )(proj3, proj3, proj3, proj3, proj3, proj3)
    return o_f.reshape(t, d), o_b.reshape(t, d)


def _gelu_tanh(y):
    return 0.5 * y * (1.0 + jnp.tanh(math.sqrt(2.0 / math.pi) * (y + 0.044715 * (y * y * y))))


def _s5n_coef_kernel(a_ref, bt_ref, c_ref, k2_ref, e8_ref, w8_ref, sc_ref, lam_ref):
    L, ch, p, R = S5_L, S5_GROUP_CH, S5_STATE, S5_R
    lanes = 2 * p
    hi = lax.Precision.HIGHEST
    e8_ref[...] = jnp.zeros_like(e8_ref)
    w8_ref[...] = jnp.zeros_like(w8_ref)
    for g8 in range(S5_LB):
        a = a_ref[g8]
        cr, ci = c_ref[g8, 0], c_ref[g8, 1]
        br, bi = bt_ref[g8, 0], bt_ref[g8, 1]
        zb, ang_dt = [], []
        for dr in range(2):
            ar = a[3 * dr:3 * dr + 1]
            ai = a[3 * dr + 1:3 * dr + 2]
            dt = jnp.exp(a[3 * dr + 2:3 * dr + 3])
            mag = jnp.exp(ar * dt)
            ang = ai * dt
            lr = mag * jnp.cos(ang)
            li = mag * jnp.sin(ang)
            den = ar * ar + ai * ai
            zr = ((lr - 1.0) * ar + li * ai) / den
            zi = (li * ar - (lr - 1.0) * ai) / den
            zb.append((zr * br - zi * bi, zr * bi + zi * br))
            ang_dt.append((ar * dt, ang))

        def powers(dr, m):
            lre, lang = ang_dt[dr]
            mg = jnp.exp(m * lre)
            return mg * jnp.cos(m * lang), mg * jnp.sin(m * lang)

        def outer(pw, coef, sign=1.0):
            pr, pi = pw
            fr, fi = coef
            r = pr.shape[0]
            re = pr[:, None, :] * fr[None] - pi[:, None, :] * fi[None]
            im = pr[:, None, :] * fi[None] + pi[:, None, :] * fr[None]
            return jnp.concatenate([re.reshape(r * ch, p), sign * im.reshape(r * ch, p)], axis=1)

        rows = slice(g8 * ch, (g8 + 1) * ch)
        n = lax.broadcasted_iota(jnp.int32, (2 * L, 1), 0)
        ccp = [jnp.zeros((g8 * ch, lanes), F32), jnp.concatenate([cr, -ci], axis=1),
               jnp.zeros(((S5_LB - 1 - g8) * ch, lanes), F32)]
        ccp = jnp.concatenate([v for v in ccp if v.shape[0]], axis=0)
        kn = jnp.zeros((2 * L * ch, S5_LB * ch), F32)
        for dr in range(2):
            lag = ((L - 1) - n) if dr == 0 else (n - (L - 1))
            valid = jnp.logical_and(lag >= 0, n <= 2 * L - 2).astype(F32)
            pr, pi = powers(dr, jnp.maximum(lag, 0).astype(F32))
            zbp = outer((pr * valid, pi * valid), zb[dr])
            kn = kn + lax.dot_general(zbp, ccp, (((1,), (1,)), ((), ())), precision=hi,
                                      preferred_element_type=F32)
        kn3 = kn.reshape(2 * L, ch, S5_LB * ch).astype(k2_ref.dtype)
        half = S5_LB * ch
        k2_ref[0, :, rows, 0:half] = kn3
        k2_ref[0, 1:, rows, half:] = kn3[:-1]
        k2_ref[0, 0:1, rows, half:] = jnp.zeros((1, ch, half), k2_ref.dtype)

        rr = lax.broadcasted_iota(jnp.int32, (R, 1), 0).astype(F32)
        pieces = (
            (e8_ref, outer(powers(0, (R - 1) - rr), zb[0]), outer(powers(1, rr), zb[1])),
            (w8_ref, outer(powers(0, rr + 1.0), (cr, ci), -1.0), outer(powers(1, R - rr), (cr, ci), -1.0)),
        )
        for ref, fwd, bwd in pieces:
            for dr, piece in enumerate((fwd, bwd)):
                col = (2 * g8 + dr) * lanes
                ref[0, :, rows, col:col + lanes] = piece.reshape(R, ch, lanes).astype(ref.dtype)

        lane = lax.broadcasted_iota(jnp.int32, (1, lanes), 1)
        sign = jnp.where(lane < p, -1.0, 1.0)

        def mul_rows(dr, m):
            pr, pi = powers(dr, jnp.full((1, 1), float(m), F32))
            return jnp.concatenate([pr, pr], axis=1), jnp.concatenate([pi, pi], axis=1) * sign

        na = L // R
        for a_i in range(na):
            for dr, m in ((0, R * a_i), (1, R * (na - 1 - a_i))):
                col = (2 * g8 + dr) * lanes
                ca, cb = mul_rows(dr, m)
                sc_ref[0, a_i:a_i + 1, col:col + lanes] = ca
                sc_ref[0, na + a_i:na + a_i + 1, col:col + lanes] = cb
        for dr in range(2):
            ca, cb = mul_rows(dr, L)
            lam_ref[0, 2 * dr, g8:g8 + 1, :] = ca
            lam_ref[0, 2 * dr + 1, g8:g8 + 1, :] = cb


def _s5n_coefs(a_re, a_im, log_dt, b_re, b_im, c_re, c_im):
    g, p, ch, L, R = S5_GROUPS, S5_STATE, S5_GROUP_CH, S5_L, S5_R
    nlb = g // S5_LB
    lw = S5_LB * ch
    sw = S5_LB * 4 * p
    ldt = jnp.broadcast_to(log_dt[:, :, None], (2, g, p))
    zero = jnp.zeros((g, p), F32)
    amat = jnp.stack([a_re[0], a_im[0], ldt[0], a_re[1], a_im[1], ldt[1], zero, zero], axis=1)
    bt = jnp.stack([b_re, b_im], axis=1).transpose(0, 1, 3, 2)
    cc = jnp.stack([c_re, c_im], axis=1)
    k2, e8, w8, sc, lam = pl.pallas_call(
        _s5n_coef_kernel,
        grid=(nlb,),
        in_specs=[
            pl.BlockSpec((S5_LB, 8, p), lambda i: (i, 0, 0)),
            pl.BlockSpec((S5_LB, 2, ch, p), lambda i: (i, 0, 0, 0)),
            pl.BlockSpec((S5_LB, 2, ch, p), lambda i: (i, 0, 0, 0)),
        ],
        out_specs=[
            pl.BlockSpec((1, 2 * L, lw, 2 * lw), lambda i: (i, 0, 0, 0)),
            pl.BlockSpec((1, R, lw, sw), lambda i: (i, 0, 0, 0)),
            pl.BlockSpec((1, R, lw, sw), lambda i: (i, 0, 0, 0)),
            pl.BlockSpec((1, 2 * (L // R), sw), lambda i: (i, 0, 0)),
            pl.BlockSpec((1, 4, S5_LB, 2 * p), lambda i: (i, 0, 0, 0)),
        ],
        out_shape=[
            jax.ShapeDtypeStruct((nlb, 2 * L, lw, 2 * lw), BF16),
            jax.ShapeDtypeStruct((nlb, R, lw, sw), BF16),
            jax.ShapeDtypeStruct((nlb, R, lw, sw), BF16),
            jax.ShapeDtypeStruct((nlb, 2 * (L // R), sw), F32),
            jax.ShapeDtypeStruct((nlb, 4, S5_LB, 2 * p), F32),
        ],
        compiler_params=_params(("parallel",), 48 * 1024 * 1024),
        name="s5_coef",
    )(amat, bt, cc)
    return k2.reshape(nlb, 2 * L * lw, 2 * lw), e8.reshape(nlb, R * lw, sw), w8.reshape(nlb, R * lw, sw), sc, lam


def _swap_halves(v):
    w = 2 * S5_STATE
    return jnp.concatenate([pltpu.roll(v[:, k * w:(k + 1) * w], S5_STATE, 1) for k in range(v.shape[1] // w)],
                           axis=1)


def _s5n_load_chunks(x_ref, m_ref, s):
    tm = x_ref.shape[0] // S5_L
    return x_ref[pl.ds(s, tm, stride=S5_L), :] * (1.0 + m_ref[0, 4:5, :]) + m_ref[0, 3:4, :]


def _s5n_in_kernel(x_ref, m_ref, e8_ref, sc_ref, e_ref):
    L, R, w = S5_L, S5_R, 2 * S5_STATE
    na = L // R
    tm = x_ref.shape[0] // L
    acc = jnp.zeros((tm, e8_ref.shape[2]), F32)
    for a in range(na):
        xa = jnp.concatenate([_s5n_load_chunks(x_ref, m_ref, R * a + r).astype(BF16) for r in range(R)], axis=1)
        ea = jnp.dot(xa, e8_ref[0], preferred_element_type=F32)
        ta = na - 1 - a
        acc = acc + sc_ref[0, ta:ta + 1, :] * ea + sc_ref[0, na + ta:na + ta + 1, :] * _swap_halves(ea)
    for plane, val in enumerate((acc, _swap_halves(acc))):
        for g8 in range(S5_LB):
            for dr in range(2):
                col = (2 * g8 + dr) * w
                e_ref[0, dr, plane, 0, pl.ds(g8, tm, stride=S5_LB), :] = val[:, col:col + w]


def _s5n_scan_kernel(e_ref, lam_ref, s_ref):
    nb = e_ref.shape[3]
    sub = S5_LB
    nch = e_ref.shape[4] // sub
    dr = pl.program_id(1)
    ca, cb = lam_ref[0, 0], lam_ref[0, 1]

    def body(j, carry):
        off = pl.multiple_of(jnp.where(dr == 0, j, nch - 1 - j) * sub, sub)
        new = []
        for b in range(nb):
            x, xs = carry[2 * b], carry[2 * b + 1]
            s_ref[0, 0, b, pl.ds(off, sub), :] = x
            new.append(ca * x + cb * xs + e_ref[0, 0, 0, b, pl.ds(off, sub), :])
            new.append(ca * xs - cb * x + e_ref[0, 0, 1, b, pl.ds(off, sub), :])
        return tuple(new)

    zero = jnp.zeros((sub, 2 * S5_STATE), F32)
    lax.fori_loop(0, nch, body, tuple([zero] * (2 * nb)))


def _s5n_out_kernel(x_ref, m_ref, d_ref, k2_ref, s_ref, w8_ref, sc_ref, y_ref):
    L, R, w = S5_L, S5_R, 2 * S5_STATE
    na = L // R
    lw = x_ref.shape[1]
    tm = x_ref.shape[0] // L
    xs = [_s5n_load_chunks(x_ref, m_ref, s) for s in range(L)]
    xflat = jnp.concatenate([v.astype(BF16) for v in xs], axis=1)
    st = jnp.concatenate([s_ref[0, dr, 0, pl.ds(g8, tm, stride=S5_LB), :]
                          for g8 in range(S5_LB) for dr in range(2)], axis=1)
    stsw = _swap_halves(st)
    corr = []
    for a in range(na):
        sa = (sc_ref[0, a:a + 1, :] * st + sc_ref[0, na + a:na + a + 1, :] * stsw).astype(BF16)
        corr.append(lax.dot_general(sa, w8_ref[0], (((1,), (1,)), ((), ())), preferred_element_type=F32))
    for tp in range(L // 2):
        start = (L - 1 - 2 * tp) * lw
        yp = jnp.dot(xflat, k2_ref[0, start:start + L * lw, :], preferred_element_type=F32)
        for q in range(2):
            t = 2 * tp + q
            y = yp[:, q * lw:(q + 1) * lw] + corr[t // R][:, (t % R) * lw:(t % R + 1) * lw] + d_ref[...] * xs[t]
            y_ref[pl.ds(t, tm, stride=L), :] = _gelu_tanh(y)


def _s5n_core(x, mods, coefs, d_skip, bsz, seq):
    k2, e8, w8, sc, lam = coefs
    t, d = x.shape
    L, p = S5_L, S5_STATE
    lw = S5_LB * S5_GROUP_CH
    nlb = d // lw
    mb = seq // L
    rows = mb * S5_LB

    x_spec = pl.BlockSpec((seq, lw), lambda gi, b: (b, gi))
    m_spec = pl.BlockSpec((1, 9, lw), lambda gi, b: (b, 0, gi))
    st_spec = pl.BlockSpec((1, 2, 1, rows, 2 * p), lambda gi, b: (gi, 0, b, 0, 0))
    st_shape = jax.ShapeDtypeStruct((nlb, 2, bsz, rows, 2 * p), F32)

    def per_block(arr):
        return pl.BlockSpec((1,) + arr.shape[1:], lambda gi, b: (gi,) + (0,) * (arr.ndim - 1))

    e = pl.pallas_call(
        _s5n_in_kernel,
        grid=(nlb, bsz),
        in_specs=[x_spec, m_spec, per_block(e8), per_block(sc)],
        out_specs=pl.BlockSpec((1, 2, 2, 1, rows, 2 * p), lambda gi, b: (gi, 0, 0, b, 0, 0)),
        out_shape=jax.ShapeDtypeStruct((nlb, 2, 2, bsz, rows, 2 * p), F32),
        compiler_params=_params(("parallel", "parallel"), 48 * 1024 * 1024),
        name="s5_in",
    )(x, mods, e8, sc)

    states = pl.pallas_call(
        _s5n_scan_kernel,
        grid=(nlb, 2),
        in_specs=[
            pl.BlockSpec((1, 1, 2, bsz, rows, 2 * p), lambda gi, dr: (gi, dr, 0, 0, 0, 0)),
            pl.BlockSpec((1, 2, S5_LB, 2 * p), lambda gi, dr: (gi, dr, 0, 0)),
        ],
        out_specs=pl.BlockSpec((1, 1, bsz, rows, 2 * p), lambda gi, dr: (gi, dr, 0, 0, 0)),
        out_shape=st_shape,
        compiler_params=_params(("parallel", "parallel")),
        name="s5_scan",
    )(e, lam)

    return pl.pallas_call(
        _s5n_out_kernel,
        grid=(nlb, bsz),
        in_specs=[x_spec, m_spec, pl.BlockSpec((1, lw), lambda gi, b: (0, gi)),
                  per_block(k2), st_spec, per_block(w8), per_block(sc)],
        out_specs=x_spec,
        out_shape=jax.ShapeDtypeStruct((t, d), F32),
        compiler_params=_params(("parallel", "parallel"), 52 * 1024 * 1024),
        name="s5_out",
    )(x, mods, d_skip.reshape(1, d), k2, states, w8, sc)


def _row_spec(tm, d, col=0):
    return pl.BlockSpec((tm, d), lambda i: (i, col))


def _na_mixer(x, mods, w_qkv, rpb, w_out, ln_g, ln_b, bsz, seq):
    d = D_MODEL
    scale = jnp.concatenate([jnp.full((d,), NA_HEAD_DIM ** -0.5, F32), jnp.ones((2 * d,), F32)])
    qkv = _mod_matmul(x, mods, (w_qkv * scale).astype(BF16), BF16, seq)
    return _na_core(qkv, _na_bias_table(rpb), x, mods, w_out.astype(BF16), ln_g, ln_b, bsz, seq)


def _hgrn_mixer(x, mods, w_in, lbs, norm_g, w_out, ln_g, ln_b, bsz, seq):
    d = D_MODEL
    w_bf = w_in.astype(BF16)
    proj = _mod_matmul(x, mods, w_bf[:, :4 * d], F32, seq, lbs=lbs)
    o_f, o_b = _gla(proj, bsz, seq)
    tm = min(OUT_TM, seq)
    return _out_call(_out_hgrn_kernel, "out_hgrn", [_row_spec(tm, d), _row_spec(tm, d)], [o_f, o_b], x, mods,
                     [_resident((1, d), lambda i: (0, 0)), _resident((d, d), lambda i: (0, 0))],
                     [norm_g.reshape(1, d), w_bf[:, 4 * d:]],
                     w_out.astype(BF16), ln_g, ln_b, seq)


def _s5_mixer(x, mods, coefs, d_skip, w_out, ln_g, ln_b, bsz, seq):
    d = D_MODEL
    act = _s5n_core(x, mods, coefs, d_skip, bsz, seq)
    tm = min(OUT_TM, seq)
    return _out_call(_out_s5_kernel, "out_s5", [_row_spec(tm, d)], [act], x, mods, [], [],
                     w_out.astype(BF16), ln_g, ln_b, seq)


def _hgrn_lower_bounds(lb_logits):
    p = jax.nn.softmax(lb_logits.astype(F32), axis=1)
    cs = jnp.cumsum(p, axis=1)
    return cs - cs[:, :1]


def kernel(x, c, ada_w, ada_b, ln_g, ln_b, ffn_w_in, ffn_w_out, na_w_qkv, na_rpb, na_w_out,
           hgrn_w_in, hgrn_lb_logits, hgrn_norm_g, hgrn_w_out,
           s5_a_re, s5_a_im, s5_log_dt, s5_b_re, s5_b_im, s5_c_re, s5_c_im, s5_d, s5_w_out):
    bsz, seq, d = x.shape
    depth = ada_w.shape[0]
    mods_all = _ada_mods(c, ada_w, ada_b)
    lbs = _hgrn_lower_bounds(hgrn_lb_logits)
    xf = x.reshape(bsz * seq, d)
    w_in_bf = ffn_w_in.astype(BF16)
    w_out_bf = ffn_w_out.astype(BF16)
    for i in range(depth):
        kind = i % N_MIXERS
        j = i // N_MIXERS
        mods = mods_all[i]
        xf = _ffn(xf, mods, w_in_bf, w_out_bf, i, 0, ln_g[i, 0], ln_b[i, 0], 0, seq)
        if kind == 0:
            xf = _na_mixer(xf, mods, na_w_qkv[j], na_rpb[j], na_w_out[j], ln_g[i, 1], ln_b[i, 1], bsz, seq)
        elif kind == 1:
            xf = _hgrn_mixer(xf, mods, hgrn_w_in[j], lbs[:, i], hgrn_norm_g[j], hgrn_w_out[j],
                             ln_g[i, 1], ln_b[i, 1], bsz, seq)
        else:
            coefs = _s5n_coefs(s5_a_re[j], s5_a_im[j], s5_log_dt[j], s5_b_re[j], s5_b_im[j],
                               s5_c_re[j], s5_c_im[j])
            xf = _s5_mixer(xf, mods, coefs, s5_d[j], s5_w_out[j], ln_g[i, 1], ln_b[i, 1], bsz, seq)
        xf = _ffn(xf, mods, w_in_bf, w_out_bf, i, 1, ln_g[i, 2], ln_b[i, 2], 2, seq)
    return xf.reshape(bsz, seq, d)
```
